```python
import math
import jax, jax.numpy as jnp
from jax import lax
import numpy as np


D_MODEL = 4096
BATCH = 4
SEQ = 2048
DEPTH = 1
DEC_BATCH = 32
DEC_SEQ = 8
PAST_LEN = 8192
PAGE_SIZE = 128

GLA_HEADS = 4
GLA_DV = D_MODEL // (2 * GLA_HEADS)
GLA_DK = GLA_DV // 2
GLA_LOWRANK = 16
GLA_GATE_TEMP = 16.0
GLA_CHUNK = 64
DIFF_HEADS = 8
DIFF_DV = D_MODEL // (2 * DIFF_HEADS)
DIFF_DK = DIFF_DV // 2
ATTN_BLOCK = 128
PEER_HEADS = 8
PEER_NKEYS = 128
PEER_NEXPERTS = PEER_NKEYS * PEER_NKEYS
PEER_DQ = 128
PEER_TOPK = 16
PEER_BLOCK = 128
RMS_EPS = 1e-6

GLA_QK_W = GLA_HEADS * GLA_DK
GLA_V_W = GLA_HEADS * GLA_DV
DIFF_QK_W = DIFF_HEADS * 2 * DIFF_DK
DIFF_V_W = DIFF_HEADS * DIFF_DV
SPLIT_SIZES = (GLA_QK_W, GLA_QK_W, GLA_V_W, GLA_V_W, GLA_LOWRANK, DIFF_QK_W, DIFF_QK_W, DIFF_V_W)
W_IN_COLS = sum(SPLIT_SIZES)

kernel_name = 'hymba_gla_diffattn_peer_step'


def _rms(x, g):
    xf = x.astype(jnp.float32)
    y = xf * lax.rsqrt(jnp.mean(xf * xf, axis=-1, keepdims=True) + RMS_EPS)
    return (y * g.astype(jnp.float32)).astype(x.dtype)


def _alibi_slopes(n):
    return jnp.asarray([2.0 ** (-8.0 * (i + 1) / n) for i in range(n)], jnp.float32)


def _gla(q, k, v, log_a, s0):
    B, T, H, DK = q.shape
    DV = v.shape[-1]
    C = GLA_CHUNK if T % GLA_CHUNK == 0 else T
    nc = T // C

    def chunks(a):
        return a.astype(jnp.float32).reshape(B, nc, C, *a.shape[2:]).swapaxes(0, 1)

    causal = jnp.tril(jnp.ones((C, C), bool))

    def step(S, inp):
        qc, kc, vc, gc = inp
        G = jnp.cumsum(gc, axis=1)
        o_inter = jnp.einsum('bthk,bhkv->bthv', qc * jnp.exp(G), S)
        dG = G[:, :, None] - G[:, None, :]
        decay = jnp.exp(jnp.where(causal[None, :, :, None, None], dG, -jnp.inf))
        A = jnp.sum(qc[:, :, None] * kc[:, None, :] * decay, axis=-1)
        o_intra = jnp.einsum('btsh,bshv->bthv', A, vc)
        G_last = G[:, -1]
        kd = kc * jnp.exp(G_last[:, None] - G)
        S = jnp.exp(G_last)[..., None] * S + jnp.einsum('bshk,bshv->bhkv', kd, vc)
        return S, o_inter + o_intra

    S, o = lax.scan(step, s0.astype(jnp.float32), (chunks(q), chunks(k), chunks(v), chunks(log_a)))
    o = o.swapaxes(0, 1).reshape(B, T, H, DV)
    return o.astype(v.dtype), S.astype(s0.dtype)


def _diff_attn_block(qi, pi, segs, slopes, lam):
    logits = []
    for k, _, kpos in segs:
        s = jnp.einsum('bqhcd,bkhcd->bhcqk', qi, k).astype(jnp.float32)
        dist = pi[:, None] - kpos[None, :]
        bias = slopes[:, None, None, None] * dist.astype(jnp.float32)
        logits.append(jnp.where(dist >= 0, s - bias, -jnp.inf))
    p = jax.nn.softmax(jnp.concatenate(logits, axis=-1), axis=-1)
    pd = p[:, :, 0] - lam * p[:, :, 1]
    o = None
    start = 0
    for k, v, _ in segs:
        n = k.shape[1]
        part = jnp.einsum('bhqk,bkhe->bqhe', pd[..., start:start + n].astype(v.dtype), v)
        o = part if o is None else o + part
        start += n
    return o


def _diff_attention(q, q_pos, segs, slopes, lam):
    B, T, H, _, DK = q.shape
    blk = ATTN_BLOCK if T % ATTN_BLOCK == 0 else T
    nb = T // blk
    qb = q.reshape(B, nb, blk, H, 2, DK).swapaxes(0, 1)
    pb = q_pos.reshape(nb, blk)
    o = lax.map(lambda a: _diff_attn_block(a[0], a[1], segs, slopes, lam), (qb, pb))
    return o.swapaxes(0, 1).reshape(B, T, H, DIFF_DV)


def _mixers(xn, pos0, k_past, v_past, s0, lam_init, w_in, w_a2, b_a, gla_gain,
            lq1, lk1, lq2, lk2, diff_gain, w_o):
    B, T, _ = xn.shape
    proj = jnp.einsum('btd,de->bte', xn, w_in)
    points = np.cumsum(SPLIT_SIZES)[:-1].tolist()
    gq, gk, gv, gr, ga, dq, dk, dv = jnp.split(proj, points, axis=-1)
    gq = gq.reshape(B, T, GLA_HEADS, GLA_DK) * (GLA_DK ** -0.5)
    gk = gk.reshape(B, T, GLA_HEADS, GLA_DK)
    gv = gv.reshape(B, T, GLA_HEADS, GLA_DV)
    log_a = jax.nn.log_sigmoid((jnp.einsum('btr,rk->btk', ga, w_a2) + b_a).astype(jnp.float32)) / GLA_GATE_TEMP
    log_a = log_a.reshape(B, T, GLA_HEADS, GLA_DK)
    o_g, s_new = _gla(gq, gk, gv, log_a, s0)
    o_g = _rms(o_g, gla_gain.reshape(GLA_HEADS, GLA_DV)).reshape(B, T, GLA_V_W) * jax.nn.silu(gr)
    dq = dq.reshape(B, T, DIFF_HEADS, 2, DIFF_DK) * (DIFF_DK ** -0.5)
    dk = dk.reshape(B, T, DIFF_HEADS, 2, DIFF_DK)
    dv = dv.reshape(B, T, DIFF_HEADS, DIFF_DV)
    q_pos = pos0 + jnp.arange(T, dtype=jnp.int32)
    segs = ((dk, dv, q_pos),)
    if k_past is not None:
        segs = ((k_past, v_past, jnp.arange(k_past.shape[1], dtype=jnp.int32)),) + segs
    f32 = jnp.float32
    lam = (jnp.exp(jnp.sum(lq1.astype(f32) * lk1.astype(f32)))
           - jnp.exp(jnp.sum(lq2.astype(f32) * lk2.astype(f32))) + lam_init)
    o_d = _diff_attention(dq, q_pos, segs, _alibi_slopes(DIFF_HEADS), lam)
    o_d = (_rms(o_d, diff_gain) * (1.0 - lam_init)).reshape(B, T, DIFF_V_W).astype(o_g.dtype)
    mixed = jnp.einsum('btm,md->btd', jnp.concatenate([o_g, o_d], axis=-1), w_o)
    return mixed, dk, dv, s_new


def _peer(xn, wq, keys1, keys2, u, v):
    B, T, D = xn.shape
    n = B * T
    xf = xn.reshape(n, D)
    q = jnp.einsum('nd,dq->nq', xf, wq).reshape(n, PEER_HEADS, 2, PEER_DQ // 2).astype(jnp.float32)
    s1 = jnp.einsum('nhd,kd->nhk', q[:, :, 0], keys1.astype(jnp.float32))
    s2 = jnp.einsum('nhd,kd->nhk', q[:, :, 1], keys2.astype(jnp.float32))
    t1, i1 = lax.top_k(s1, PEER_TOPK)
    t2, i2 = lax.top_k(s2, PEER_TOPK)
    cand = (t1[..., :, None] + t2[..., None, :]).reshape(n, PEER_HEADS, PEER_TOPK * PEER_TOPK)
    cidx = (i1[..., :, None] * PEER_NKEYS + i2[..., None, :]).reshape(n, PEER_HEADS, PEER_TOPK * PEER_TOPK)
    best, pos = lax.top_k(cand, PEER_TOPK)
    eidx = jnp.take_along_axis(cidx, pos, axis=-1).reshape(n, PEER_HEADS * PEER_TOPK)
    gate = jax.nn.softmax(best, axis=-1).reshape(n, PEER_HEADS * PEER_TOPK).astype(xn.dtype)
    blk = min(PEER_BLOCK, n)
    pad = (-n) % blk
    nb = (n + pad) // blk
    xp = jnp.pad(xf, ((0, pad), (0, 0))).reshape(nb, blk, D)
    ip = jnp.pad(eidx, ((0, pad), (0, 0))).reshape(nb, blk, -1)
    gp = jnp.pad(gate, ((0, pad), (0, 0))).reshape(nb, blk, -1)

    def one(a):
        xb, ib, gb = a
        pre = jnp.einsum('nd,ned->ne', xb, u[ib])
        return jnp.einsum('ne,ned->nd', jax.nn.gelu(pre) * gb, v[ib])

    out = lax.map(one, (xp, ip, gp))
    return out.reshape(nb * blk, D)[:n].reshape(B, T, D)


def _block(x, pos0, k_past, v_past, s0, lam_init, norm1_gain, w_in, w_a2, b_a, gla_gain,
           lq1, lk1, lq2, lk2, diff_gain, w_o, norm2_gain, peer_wq, peer_keys1, peer_keys2,
           peer_u, peer_v):
    mixed, k_new, v_new, s_new = _mixers(_rms(x, norm1_gain), pos0, k_past, v_past, s0, lam_init,
                                         w_in, w_a2, b_a, gla_gain, lq1, lk1, lq2, lk2, diff_gain, w_o)
    h = x + mixed
    h = h + _peer(_rms(h, norm2_gain), peer_wq, peer_keys1, peer_keys2, peer_u, peer_v)
    return h, k_new, v_new, s_new


def setup_inputs(seed: int = 0) -> dict:
    key = jax.random.key(seed)
    ks = jax.random.split(key, 24)
    f32 = jnp.float32
    n_pages = PAST_LEN // PAGE_SIZE
    n_used = DEC_BATCH * n_pages
    n_pool = n_used + max(1, n_used // 4)

    def nrm(k, shape, s):
        return jax.random.normal(k, shape, f32) * s

    page_table = jax.random.permutation(ks[5], n_pool)[:n_used].reshape(DEC_BATCH, n_pages).astype(jnp.int32)
    return {
        'x_prompt': nrm(ks[0], (BATCH, SEQ, D_MODEL), 1.0),
        'x_sample': nrm(ks[1], (DEC_BATCH, DEC_SEQ, D_MODEL), 1.0),
        'cache_k': nrm(ks[2], (DEPTH, n_pool, PAGE_SIZE, DIFF_HEADS, 2, DIFF_DK), 1.0),
        'cache_v': nrm(ks[3], (DEPTH, n_pool, PAGE_SIZE, DIFF_HEADS, DIFF_DV), 1.0),
        'state_gla': nrm(ks[4], (DEPTH, DEC_BATCH, GLA_HEADS, GLA_DK, GLA_DV), 1.0),
        'page_table': page_table,
        'norm1_gain': 1.0 + nrm(ks[6], (DEPTH, D_MODEL), 0.02),
        'w_in': nrm(ks[7], (DEPTH, D_MODEL, W_IN_COLS), D_MODEL ** -0.5),
        'w_a2': nrm(ks[8], (DEPTH, GLA_LOWRANK, GLA_QK_W), GLA_LOWRANK ** -0.5),
        'b_a': nrm(ks[9], (DEPTH, GLA_QK_W), 0.1),
        'gla_gain': 1.0 + nrm(ks[10], (DEPTH, GLA_V_W), 0.02),
        'lambda_q1': nrm(ks[11], (DEPTH, DIFF_DK), 0.1),
        'lambda_k1': nrm(ks[12], (DEPTH, DIFF_DK), 0.1),
        'lambda_q2': nrm(ks[13], (DEPTH, DIFF_DK), 0.1),
        'lambda_k2': nrm(ks[14], (DEPTH, DIFF_DK), 0.1),
        'diff_gain': 1.0 + nrm(ks[15], (DEPTH, DIFF_DV), 0.02),
        'w_o': nrm(ks[16], (DEPTH, D_MODEL, D_MODEL), D_MODEL ** -0.5),
        'norm2_gain': 1.0 + nrm(ks[17], (DEPTH, D_MODEL), 0.02),
        'peer_wq': nrm(ks[18], (DEPTH, D_MODEL, PEER_HEADS * PEER_DQ), D_MODEL ** -0.5),
        'peer_keys1': nrm(ks[19], (DEPTH, PEER_NKEYS, PEER_DQ // 2), (PEER_DQ // 2) ** -0.5),
        'peer_keys2': nrm(ks[20], (DEPTH, PEER_NKEYS, PEER_DQ // 2), (PEER_DQ // 2) ** -0.5),
        'peer_u': nrm(ks[21], (DEPTH, PEER_NEXPERTS, D_MODEL), D_MODEL ** -0.5),
        'peer_v': nrm(ks[22], (DEPTH, PEER_NEXPERTS, D_MODEL), 0.5),
        'final_gain': 1.0 + nrm(ks[23], (D_MODEL,), 0.02),
    }


def reference(x_prompt, x_sample, cache_k, cache_v, state_gla, page_table, norm1_gain, w_in, w_a2,
              b_a, gla_gain, lambda_q1, lambda_k1, lambda_q2, lambda_k2, diff_gain, w_o, norm2_gain,
              peer_wq, peer_keys1, peer_keys2, peer_u, peer_v, final_gain):
    B = x_prompt.shape[0]
    DB, n_pages = page_table.shape
    past_len = n_pages * PAGE_SIZE
    xp, xs = x_prompt, x_sample
    kp_l, vp_l, sp_l, ks_l, vs_l, ss_l = [], [], [], [], [], []
    for l in range(DEPTH):
        lam_init = 0.8 - 0.6 * math.exp(-0.3 * l)
        lw = (norm1_gain[l], w_in[l], w_a2[l], b_a[l], gla_gain[l], lambda_q1[l], lambda_k1[l],
              lambda_q2[l], lambda_k2[l], diff_gain[l], w_o[l], norm2_gain[l], peer_wq[l],
              peer_keys1[l], peer_keys2[l], peer_u[l], peer_v[l])
        s0p = jnp.zeros((B, GLA_HEADS, GLA_DK, GLA_DV), x_prompt.dtype)
        xp, kp, vp, sp = _block(xp, 0, None, None, s0p, lam_init, *lw)
        k_past = cache_k[l][page_table].reshape(DB, past_len, DIFF_HEADS, 2, DIFF_DK)
        v_past = cache_v[l][page_table].reshape(DB, past_len, DIFF_HEADS, DIFF_DV)
        xs, kss, vss, sss = _block(xs, past_len, k_past, v_past, state_gla[l], lam_init, *lw)
        kp_l.append(kp); vp_l.append(vp); sp_l.append(sp)
        ks_l.append(kss); vs_l.append(vss); ss_l.append(sss)
    y_prompt = _rms(xp, final_gain)
    y_sample = _rms(xs, final_gain)
    return (y_prompt, y_sample, jnp.stack(kp_l), jnp.stack(vp_l), jnp.stack(sp_l),
            jnp.stack(ks_l), jnp.stack(vs_l), jnp.stack(ss_l))
```

```python
import functools
import math

import jax
import jax.numpy as jnp
import numpy as np
from jax import lax
from jax.experimental import pallas as pl
from jax.experimental.pallas import tpu as pltpu

F32 = jnp.float32
BF16 = jnp.bfloat16

GLA_HEADS = 4
GLA_LOWRANK = 16
GLA_GATE_TEMP = 16.0
DIFF_HEADS = 8
PEER_HEADS = 8
PEER_NKEYS = 128
PEER_DQ = 128
PEER_TOPK = 16
PAGE_SIZE = 128
RMS_EPS = 1e-6
LANES = 128
VMEM_LIMIT = 56 * 1024 * 1024

NT_DIMS = (((1,), (1,)), ((), ()))
NEG_INF = float("-inf")


def _cparams(sem):
    return pltpu.CompilerParams(dimension_semantics=sem, vmem_limit_bytes=VMEM_LIMIT)


def _rms_kernel(x_ref, g_ref, o_ref):
    x = x_ref[...]
    ms = jnp.mean(x * x, axis=-1, keepdims=True)
    o_ref[...] = (x * lax.rsqrt(ms + RMS_EPS) * g_ref[...]).astype(o_ref.dtype)


def rms_norm(x, g, tm, out_dtype=BF16):
    n, d = x.shape
    return pl.pallas_call(
        _rms_kernel,
        grid=(n // tm,),
        in_specs=[pl.BlockSpec((tm, d), lambda i: (i, 0)),
                  pl.BlockSpec((1, d), lambda i: (0, 0))],
        out_specs=pl.BlockSpec((tm, d), lambda i: (i, 0)),
        out_shape=jax.ShapeDtypeStruct((n, d), out_dtype),
        compiler_params=_cparams(("parallel",)),
        name="rms_norm",
    )(x, g.reshape(1, d))


def _mm_kernel(x_ref, w_ref, o_ref):
    o_ref[...] = jnp.dot(x_ref[...], w_ref[...], preferred_element_type=F32)


def _mm_res_kernel(x_ref, w_ref, r_ref, o_ref):
    o_ref[...] = r_ref[...] + jnp.dot(x_ref[...], w_ref[...], preferred_element_type=F32)


def matmul(x, w, tm, tn, res=None, name="matmul"):
    n, k = x.shape
    m = w.shape[1]
    in_specs = [pl.BlockSpec((tm, k), lambda i, j: (i, 0)),
                pl.BlockSpec((k, tn), lambda i, j: (0, j))]
    args = [x, w]
    kern = _mm_kernel
    if res is not None:
        in_specs.append(pl.BlockSpec((tm, tn), lambda i, j: (i, j)))
        args.append(res)
        kern = _mm_res_kernel
    return pl.pallas_call(
        kern,
        grid=(n // tm, m // tn),
        in_specs=in_specs,
        out_specs=pl.BlockSpec((tm, tn), lambda i, j: (i, j)),
        out_shape=jax.ShapeDtypeStruct((n, m), F32),
        compiler_params=_cparams(("parallel", "parallel")),
        name=name,
    )(*args)


def _mm_nt_kernel(a_ref, b_ref, o_ref):
    o_ref[...] = lax.dot_general(a_ref[...], b_ref[...], NT_DIMS, preferred_element_type=F32)


def matmul_nt(a, b, tm, tn, name="matmul_nt"):
    m, k = a.shape
    n = b.shape[0]
    return pl.pallas_call(
        _mm_nt_kernel,
        grid=(n // tn, m // tm),
        in_specs=[pl.BlockSpec((tm, k), lambda j, i: (i, 0)),
                  pl.BlockSpec((tn, k), lambda j, i: (j, 0))],
        out_specs=pl.BlockSpec((tm, tn), lambda j, i: (i, j)),
        out_shape=jax.ShapeDtypeStruct((m, n), F32),
        compiler_params=_cparams(("parallel", "parallel")),
        name=name,
    )(a, b)


def _mm_acc_kernel(a_ref, b_ref, o_ref):
    @pl.when(pl.program_id(2) == 0)
    def _():
        o_ref[...] = jnp.zeros_like(o_ref)

    o_ref[...] += jnp.dot(a_ref[...], b_ref[...], preferred_element_type=F32)


def matmul_kacc(a, b, tm, tn, tk, name="matmul_kacc"):
    m, k = a.shape
    n = b.shape[1]
    return pl.pallas_call(
        _mm_acc_kernel,
        grid=(m // tm, n // tn, k // tk),
        in_specs=[pl.BlockSpec((tm, tk), lambda i, j, kk: (i, kk)),
                  pl.BlockSpec((tk, tn), lambda i, j, kk: (kk, j))],
        out_specs=pl.BlockSpec((tm, tn), lambda i, j, kk: (i, j)),
        out_shape=jax.ShapeDtypeStruct((m, n), F32),
        compiler_params=_cparams(("parallel", "parallel", "arbitrary")),
        name=name,
    )(a, b)


GLA_SUB = 16


def _log_sigmoid(x):
    return jnp.minimum(x, 0.0) - jnp.log1p(jnp.exp(-jnp.abs(x)))


def _gla_kernel(*refs, chunk, valid, has_s0, dk, dv):
    if has_s0:
        (q_ref, k_ref, v_ref, r_ref, ga_ref, wa_ref, ba_ref, gain_ref, s0_ref,
         o_ref, s_ref, st_scr, od_scr) = refs
    else:
        (q_ref, k_ref, v_ref, r_ref, ga_ref, wa_ref, ba_ref, gain_ref,
         o_ref, s_ref, st_scr, od_scr) = refs
        s0_ref = None
    c = pl.program_id(2)
    nc = pl.num_programs(2)

    @pl.when(c == 0)
    def _():
        if has_s0:
            st_scr[...] = s0_ref[...].T
        else:
            st_scr[...] = jnp.zeros_like(st_scr)

    q = q_ref[...] * (dk ** -0.5)
    k = k_ref[...]
    v = v_ref[...]
    pre = jnp.dot(ga_ref[...], wa_ref[...], preferred_element_type=F32,
                  precision=lax.Precision.HIGHEST) + ba_ref[...]
    g = _log_sigmoid(pre) * (1.0 / GLA_GATE_TEMP)
    row = lax.broadcasted_iota(jnp.int32, (chunk, 1), 0)
    if valid < chunk:
        g = jnp.where(row < valid, g, 0.0)
    ri = lax.broadcasted_iota(jnp.int32, (chunk, chunk), 0)
    ci = lax.broadcasted_iota(jnp.int32, (chunk, chunk), 1)
    tril = (ci <= ri).astype(F32)
    G = jnp.dot(tril, g, preferred_element_type=F32, precision=lax.Precision.HIGHEST)
    g_last = G[chunk - 1:chunk, :]

    st = st_scr[...]
    st_b = st.astype(BF16)
    qg = (q * jnp.exp(G)).astype(BF16)
    o = lax.dot_general(qg, st_b, NT_DIMS, preferred_element_type=F32)

    nsub = chunk // GLA_SUB
    sub_rows = lax.broadcasted_iota(jnp.int32, (GLA_SUB, 1), 0)
    row_ids = lax.broadcasted_iota(jnp.int32, (GLA_SUB, dv), 0)
    for i in range(nsub):
        lo = i * GLA_SUB
        hi = lo + GLA_SUB
        Gi = G[lo:hi, :]
        qi = q[lo:hi, :]
        ki = k[lo:hi, :]
        vi = v[lo:hi, :]
        od = jnp.zeros((GLA_SUB, dv), F32)
        for t in range(GLA_SUB):
            w = ki * jnp.exp(jnp.minimum(Gi[t:t + 1, :] - Gi, 0.0)) * qi[t:t + 1, :]
            a_col = jnp.sum(w, axis=1, keepdims=True)
            a_col = jnp.where(sub_rows <= t, a_col, 0.0)
            o_row = jnp.sum(a_col * vi, axis=0, keepdims=True)
            od = jnp.where(row_ids == t, o_row, od)
        if i > 0:
            ref_i = G[lo - 1:lo, :]
            qs = (qi * jnp.exp(Gi - ref_i)).astype(BF16)
            ks = (k[:lo, :] * jnp.exp(ref_i - G[:lo, :])).astype(BF16)
            a_off = lax.dot_general(qs, ks, NT_DIMS, preferred_element_type=F32)
            od = od + jnp.dot(a_off.astype(BF16), v[:lo, :].astype(BF16), preferred_element_type=F32)
        od_scr[lo:hi, :] = od
    o = o + od_scr[...]

    kd = (k * jnp.exp(g_last - G)).astype(BF16)
    upd = lax.dot_general(v.astype(BF16), kd, (((0,), (0,)), ((), ())), preferred_element_type=F32)
    st_new = st * jnp.exp(g_last) + upd
    st_scr[...] = st_new

    ms = jnp.mean(o * o, axis=-1, keepdims=True)
    r = r_ref[...]
    o = o * lax.rsqrt(ms + RMS_EPS) * gain_ref[...] * (r * jax.nn.sigmoid(r))
    o_ref[...] = o.astype(o_ref.dtype)

    @pl.when(c == nc - 1)
    def _():
        s_ref[...] = st_new.T


def gla(p, ga, wa2p, b_a, gain, s0, *, nbatch, seqlen, chunk, valid, row0, col_q, col_k, col_v, col_r,
        dk, dv, out_dtype, name):
    nc = seqlen // chunk
    rb0 = row0 // chunk
    h_n = GLA_HEADS
    has_s0 = s0 is not None

    def rowblk(b, h, c):
        return rb0 + b * nc + c

    in_specs = [
        pl.BlockSpec((chunk, dk), lambda b, h, c: (rowblk(b, h, c), col_q // dk + h)),
        pl.BlockSpec((chunk, dk), lambda b, h, c: (rowblk(b, h, c), col_k // dk + h)),
        pl.BlockSpec((chunk, dv), lambda b, h, c: (rowblk(b, h, c), col_v // dv + h)),
        pl.BlockSpec((chunk, dv), lambda b, h, c: (rowblk(b, h, c), col_r // dv + h)),
        pl.BlockSpec((chunk, LANES), lambda b, h, c: (rowblk(b, h, c), 0)),
        pl.BlockSpec((LANES, dk), lambda b, h, c: (0, h)),
        pl.BlockSpec((1, dk), lambda b, h, c: (0, h)),
        pl.BlockSpec((1, dv), lambda b, h, c: (0, h)),
    ]
    args = [p, p, p, p, ga, wa2p, b_a, gain]
    if has_s0:
        in_specs.append(pl.BlockSpec((None, None, dk, dv), lambda b, h, c: (b, h, 0, 0)))
        args.append(s0)
    kern = functools.partial(_gla_kernel, chunk=chunk, valid=valid, has_s0=has_s0, dk=dk, dv=dv)
    return pl.pallas_call(
        kern,
        grid=(nbatch, h_n, nc),
        in_specs=in_specs,
        out_specs=[pl.BlockSpec((chunk, dv), lambda b, h, c: (b * nc + c, h)),
                   pl.BlockSpec((None, None, dk, dv), lambda b, h, c: (b, h, 0, 0))],
        out_shape=[jax.ShapeDtypeStruct((nbatch * seqlen, h_n * dv), out_dtype),
                   jax.ShapeDtypeStruct((nbatch, h_n, dk, dv), F32)],
        scratch_shapes=[pltpu.VMEM((dv, dk), F32), pltpu.VMEM((chunk, dv), F32)],
        compiler_params=_cparams(("parallel", "parallel", "arbitrary")),
        name=name,
    )(*args)


def _attn_prompt_kernel(slopes_ref, lam_ref, q_ref, k_ref, v_ref, gain_ref, o_ref,
                        m_scr, l_scr, acc_scr, *, tq, tk, dk, dv, out_scale):
    h = pl.program_id(1)
    qi = pl.program_id(2)
    ki = pl.program_id(3)

    @pl.when(ki == 0)
    def _():
        m_scr[...] = jnp.full_like(m_scr, NEG_INF)
        l_scr[...] = jnp.zeros_like(l_scr)
        acc_scr[...] = jnp.zeros_like(acc_scr)

    @pl.when(ki <= qi)
    def _():
        slope = slopes_ref[h]
        qpos = qi * tq + lax.broadcasted_iota(jnp.int32, (tq, tk), 0)
        kpos = ki * tk + lax.broadcasted_iota(jnp.int32, (tq, tk), 1)
        dist = qpos - kpos
        bias = slope * dist.astype(F32)
        keep = dist >= 0
        vb = v_ref[...].astype(BF16)
        for c in range(2):
            qc = (q_ref[:, c * dk:(c + 1) * dk] * (dk ** -0.5)).astype(BF16)
            kc = k_ref[:, c * dk:(c + 1) * dk].astype(BF16)
            s = lax.dot_general(qc, kc, NT_DIMS, preferred_element_type=F32)
            s = jnp.where(keep, s - bias, NEG_INF)
            m_prev = m_scr[c]
            m_new = jnp.maximum(m_prev, jnp.max(s, axis=-1, keepdims=True))
            alpha = jnp.exp(m_prev - m_new)
            p = jnp.exp(s - m_new)
            l_scr[c] = alpha * l_scr[c] + jnp.sum(p, axis=-1, keepdims=True)
            acc_scr[c] = alpha * acc_scr[c] + jnp.dot(p.astype(BF16), vb, preferred_element_type=F32)
            m_scr[c] = m_new

    @pl.when(ki == qi)
    def _():
        lam = lam_ref[0]
        o = acc_scr[0] / l_scr[0] - lam * (acc_scr[1] / l_scr[1])
        ms = jnp.mean(o * o, axis=-1, keepdims=True)
        o = o * lax.rsqrt(ms + RMS_EPS) * gain_ref[...] * out_scale
        o_ref[...] = o.astype(o_ref.dtype)


def attn_prompt(p, slopes, lam, gain, *, nbatch, seqlen, tq, tk, col_q, col_k, col_v, dk, dv, out_scale):
    nq = seqlen // tq
    nk = seqlen // tk
    assert tq == tk
    kern = functools.partial(_attn_prompt_kernel, tq=tq, tk=tk, dk=dk, dv=dv, out_scale=out_scale)
    smem = pl.BlockSpec(memory_space=pltpu.SMEM)
    return pl.pallas_call(
        kern,
        grid=(nbatch, DIFF_HEADS, nq, nk),
        in_specs=[smem, smem,
                  pl.BlockSpec((tq, 2 * dk), lambda b, h, i, j: (b * nq + i, col_q // (2 * dk) + h)),
                  pl.BlockSpec((tk, 2 * dk), lambda b, h, i, j: (b * nk + jnp.minimum(i, j), col_k // (2 * dk) + h)),
                  pl.BlockSpec((tk, dv), lambda b, h, i, j: (b * nk + jnp.minimum(i, j), col_v // dv + h)),
                  pl.BlockSpec((1, dv), lambda b, h, i, j: (0, 0))],
        out_specs=pl.BlockSpec((tq, dv), lambda b, h, i, j: (b * nq + i, h)),
        out_shape=jax.ShapeDtypeStruct((nbatch * seqlen, DIFF_HEADS * dv), BF16),
        scratch_shapes=[pltpu.VMEM((2, tq, 1), F32), pltpu.VMEM((2, tq, 1), F32),
                        pltpu.VMEM((2, tq, dv), F32)],
        compiler_params=_cparams(("parallel", "parallel", "parallel", "arbitrary")),
        name="attn_prompt",
    )(slopes, lam, p, p, p, gain)


def _attn_sample_kernel(pt_ref, slopes_ref, lam_ref, q_ref, kn_ref, vn_ref, gain_ref, *rest,
                        pg, tnew, past_len, dk, dv, out_scale):
    k_refs = rest[:pg]
    v_refs = rest[pg:2 * pg]
    o_ref = rest[2 * pg]
    qbd_scr, m_scr, l_scr, acc_scr = rest[2 * pg + 1:]
    j = pl.program_id(1)
    nj = pl.num_programs(1)
    nh = DIFF_HEADS
    rows = 2 * tnew
    qidx = lax.broadcasted_iota(jnp.int32, (rows, 1), 0) % tnew
    qpos = past_len + qidx

    @pl.when(j == 0)
    def _():
        lane = lax.broadcasted_iota(jnp.int32, (tnew, 2 * dk), 1)
        kidx = lax.broadcasted_iota(jnp.int32, (rows, tnew), 1)
        for h in range(nh):
            qh = q_ref[:, h * 2 * dk:(h + 1) * 2 * dk] * (dk ** -0.5)
            qbd = jnp.concatenate([jnp.where(lane < dk, qh, 0.0), jnp.where(lane >= dk, qh, 0.0)], axis=0)
            qbd_scr[h] = qbd.astype(BF16)
            kn = kn_ref[:, h * 2 * dk:(h + 1) * 2 * dk]
            s = lax.dot_general(qbd, kn, NT_DIMS, preferred_element_type=F32)
            dist = qidx - kidx
            s = jnp.where(dist >= 0, s - slopes_ref[h] * dist.astype(F32), NEG_INF)
            m = jnp.max(s, axis=-1, keepdims=True)
            p = jnp.exp(s - m)
            m_scr[h] = m
            l_scr[h] = jnp.sum(p, axis=-1, keepdims=True)
            acc_scr[h] = jnp.dot(p, vn_ref[:, h * dv:(h + 1) * dv], preferred_element_type=F32)

    lane_k = lax.broadcasted_iota(jnp.int32, (rows, pg * PAGE_SIZE), 1)
    dist = (qpos - (j * (pg * PAGE_SIZE) + lane_k)).astype(F32)
    for h in range(nh):
        qbd = qbd_scr[h]
        parts = [lax.dot_general(qbd, k_refs[i][:, h * 2 * dk:(h + 1) * 2 * dk].astype(BF16), NT_DIMS,
                                 preferred_element_type=F32) for i in range(pg)]
        s = jnp.concatenate(parts, axis=-1) - slopes_ref[h] * dist
        m_prev = m_scr[h]
        m_new = jnp.maximum(m_prev, jnp.max(s, axis=-1, keepdims=True))
        alpha = jnp.exp(m_prev - m_new)
        p = jnp.exp(s - m_new)
        l_scr[h] = alpha * l_scr[h] + jnp.sum(p, axis=-1, keepdims=True)
        pb = p.astype(BF16)
        acc = alpha * acc_scr[h]
        for i in range(pg):
            acc = acc + jnp.dot(pb[:, i * PAGE_SIZE:(i + 1) * PAGE_SIZE],
                                v_refs[i][:, h * dv:(h + 1) * dv].astype(BF16), preferred_element_type=F32)
        acc_scr[h] = acc
        m_scr[h] = m_new

    @pl.when(j == nj - 1)
    def _():
        lam = lam_ref[0]
        for h in range(nh):
            on = acc_scr[h] / l_scr[h]
            o = on[:tnew, :] - lam * on[tnew:, :]
            ms = jnp.mean(o * o, axis=-1, keepdims=True)
            o_ref[:, h * dv:(h + 1) * dv] = o * lax.rsqrt(ms + RMS_EPS) * gain_ref[...] * out_scale


def attn_sample(p, cache_k, cache_v, page_table, slopes, lam, gain, *, nbatch, tnew, row0, pg,
                colblk_q, colblk_k, colblk_v, dk, dv, out_scale):
    n_pages = page_table.shape[1]
    nj = n_pages // pg
    width = DIFF_HEADS * dv
    rb0 = row0 // tnew
    kern = functools.partial(_attn_sample_kernel, pg=pg, tnew=tnew, past_len=n_pages * PAGE_SIZE,
                             dk=dk, dv=dv, out_scale=out_scale)
    smem = pl.BlockSpec(memory_space=pltpu.SMEM)

    def page_spec(i):
        return pl.BlockSpec((None, PAGE_SIZE, width), lambda b, j, pt: (pt[b, j * pg + i], 0, 0))

    in_specs = [smem, smem,
                pl.BlockSpec((tnew, width), lambda b, j, pt: (rb0 + b, colblk_q)),
                pl.BlockSpec((tnew, width), lambda b, j, pt: (rb0 + b, colblk_k)),
                pl.BlockSpec((tnew, width), lambda b, j, pt: (rb0 + b, colblk_v)),
                pl.BlockSpec((1, dv), lambda b, j, pt: (0, 0))]
    in_specs += [page_spec(i) for i in range(pg)] + [page_spec(i) for i in range(pg)]
    grid_spec = pltpu.PrefetchScalarGridSpec(
        num_scalar_prefetch=1,
        grid=(nbatch, nj),
        in_specs=in_specs,
        out_specs=pl.BlockSpec((tnew, width), lambda b, j, pt: (b, 0)),
        scratch_shapes=[pltpu.VMEM((DIFF_HEADS, 2 * tnew, 2 * dk), BF16),
                        pltpu.VMEM((DIFF_HEADS, 2 * tnew, 1), F32),
                        pltpu.VMEM((DIFF_HEADS, 2 * tnew, 1), F32),
                        pltpu.VMEM((DIFF_HEADS, 2 * tnew, dv), F32)],
    )
    return pl.pallas_call(
        kern,
        grid_spec=grid_spec,
        out_shape=jax.ShapeDtypeStruct((nbatch * tnew, width), F32),
        compiler_params=_cparams(("parallel", "arbitrary")),
        name="attn_sample",
    )(page_table, slopes, lam, p, p, p, gain, *([cache_k] * pg), *([cache_v] * pg))


N_TOP = PEER_TOPK + 1


def _top_values(x, n):
    cur = jnp.max(x, axis=0, keepdims=True)
    vals = [cur]
    for _ in range(n - 1):
        cur = jnp.max(jnp.where(x < cur, x, NEG_INF), axis=0, keepdims=True)
        vals.append(cur)
    return vals


_CAND_PAIRS = [(a, b) for a in range(N_TOP) for b in range(N_TOP) if (a + 1) * (b + 1) <= N_TOP]
_CAND_ROWS = -(-len(_CAND_PAIRS) // 8) * 8


def _route_kernel(qt_ref, k1_ref, k2_ref, a1_ref, s2_ref, th_ref, cand_scr):
    half = PEER_DQ // 2
    tn = qt_ref.shape[1]
    cand_scr[...] = jnp.full_like(cand_scr, NEG_INF)
    for h in range(PEER_HEADS):
        q1 = qt_ref[h * PEER_DQ:h * PEER_DQ + half, :]
        q2 = qt_ref[h * PEER_DQ + half:(h + 1) * PEER_DQ, :]
        s1 = jnp.dot(k1_ref[...], q1, preferred_element_type=F32, precision=lax.Precision.HIGHEST)
        s2 = jnp.dot(k2_ref[...], q2, preferred_element_type=F32, precision=lax.Precision.HIGHEST)
        t1 = _top_values(s1, N_TOP)
        t2 = _top_values(s2, N_TOP)
        for i, (a, b) in enumerate(_CAND_PAIRS):
            cand_scr[i:i + 1, :] = t1[a] + t2[b]
        c = _top_values(cand_scr[...], N_TOP)
        m = c[0]
        z = jnp.zeros_like(m)
        for i in range(PEER_TOPK):
            z = z + jnp.exp(c[i] - m)
        shift = m + jnp.log(z)
        thr = 0.5 * (c[PEER_TOPK - 1] + c[PEER_TOPK]) - shift
        a1_ref[h] = s1 - shift
        s2_ref[h] = s2
        th_ref[h:h + 1, :] = thr


def peer_route(qt, keys1, keys2, tn):
    n = qt.shape[1]
    nk = PEER_NKEYS
    return pl.pallas_call(
        _route_kernel,
        grid=(n // tn,),
        in_specs=[pl.BlockSpec((PEER_HEADS * PEER_DQ, tn), lambda i: (0, i)),
                  pl.BlockSpec((nk, PEER_DQ // 2), lambda i: (0, 0)),
                  pl.BlockSpec((nk, PEER_DQ // 2), lambda i: (0, 0))],
        out_specs=[pl.BlockSpec((PEER_HEADS, nk, tn), lambda i: (0, 0, i)),
                   pl.BlockSpec((PEER_HEADS, nk, tn), lambda i: (0, 0, i)),
                   pl.BlockSpec((PEER_HEADS, tn), lambda i: (0, i))],
        out_shape=[jax.ShapeDtypeStruct((PEER_HEADS, nk, n), F32),
                   jax.ShapeDtypeStruct((PEER_HEADS, nk, n), F32),
                   jax.ShapeDtypeStruct((PEER_HEADS, n), F32)],
        scratch_shapes=[pltpu.VMEM((_CAND_ROWS, tn), F32)],
        compiler_params=_cparams(("parallel",)),
        name="peer_route",
    )(qt, keys1, keys2)


def _gelu_tanh(x):
    return 0.5 * x * (1.0 + jnp.tanh(math.sqrt(2.0 / math.pi) * (x + 0.044715 * (x * x * x))))


def _peer_act_kernel(x_ref, u_ref, s2_ref, a1_ref, th_ref, o_ref, pre_scr, *, tchunk):
    te, tt = o_ref.shape
    nk = PEER_NKEYS
    pre_scr[...] = lax.dot_general(u_ref[...], x_ref[...], NT_DIMS, preferred_element_type=F32)

    def row_body(r, carry):
        r0 = pl.multiple_of(r * nk, nk)
        for tc in range(tt // tchunk):
            ts = slice(tc * tchunk, (tc + 1) * tchunk)
            w = jnp.zeros((nk, tchunk), F32)
            for h in range(PEER_HEADS):
                d = s2_ref[h, :, ts] + a1_ref[h, pl.ds(r, 1), ts]
                w = w + jnp.where(d > th_ref[h:h + 1, ts], jnp.exp(d), 0.0)
            x = pre_scr[pl.ds(r0, nk), ts]
            o_ref[pl.ds(r0, nk), ts] = (_gelu_tanh(x) * w).astype(o_ref.dtype)
        return carry

    lax.fori_loop(0, te // nk, row_body, 0)


def peer_act(xn, u, a1t, s2t, tht, *, tt, te, tchunk):
    n, d = xn.shape
    ne = u.shape[0]
    rows = te // PEER_NKEYS
    kern = functools.partial(_peer_act_kernel, tchunk=tchunk)
    return pl.pallas_call(
        kern,
        grid=(n // tt, ne // te),
        in_specs=[pl.BlockSpec((tt, d), lambda t, e: (t, 0)),
                  pl.BlockSpec((te, d), lambda t, e: (e, 0)),
                  pl.BlockSpec((PEER_HEADS, PEER_NKEYS, tt), lambda t, e: (0, 0, t)),
                  pl.BlockSpec((PEER_HEADS, rows, tt), lambda t, e: (0, e, t)),
                  pl.BlockSpec((PEER_HEADS, tt), lambda t, e: (0, t))],
        out_specs=pl.BlockSpec((te, tt), lambda t, e: (e, t)),
        out_shape=jax.ShapeDtypeStruct((ne, n), BF16),
        scratch_shapes=[pltpu.VMEM((te, tt), F32)],
        compiler_params=_cparams(("parallel", "arbitrary")),
        name="peer_act",
    )(xn, u, s2t, a1t, tht)


def _final_kernel(h_ref, pt_ref, g_ref, o_ref):
    x = h_ref[...] + pt_ref[...].T
    ms = jnp.mean(x * x, axis=-1, keepdims=True)
    o_ref[...] = x * lax.rsqrt(ms + RMS_EPS) * g_ref[...]


def final_norm(h, peer_t, g, *, row0, nrows, tm):
    d = h.shape[1]
    rb0 = row0 // tm
    return pl.pallas_call(
        _final_kernel,
        grid=(nrows // tm,),
        in_specs=[pl.BlockSpec((tm, d), lambda i: (rb0 + i, 0)),
                  pl.BlockSpec((d, tm), lambda i: (0, rb0 + i)),
                  pl.BlockSpec((1, d), lambda i: (0, 0))],
        out_specs=pl.BlockSpec((tm, d), lambda i: (i, 0)),
        out_shape=jax.ShapeDtypeStruct((nrows, d), F32),
        compiler_params=_cparams(("parallel",)),
        name="final_norm",
    )(h, peer_t, g.reshape(1, d))


def _tiles(n, n_s):
    def pick(cands, m=n):
        for c in cands:
            if m % c == 0:
                return c
        raise ValueError(f"no tile for {m}")
    return dict(
        rms_tm=pick((352, 256, 128, 64, 32, 16)),
        mm_tm=pick((768, 512, 256, 128, 64, 32, 16)),
        peer_tt=pick((768, 512, 256, 128)),
        route_tn=pick((256, 128)),
        final_tm=pick((256, 128), math.gcd(n, n_s)),
    )


def _alibi_slopes(nheads):
    return jnp.asarray([2.0 ** (-8.0 * (i + 1) / nheads) for i in range(nheads)], F32)


def kernel(x_prompt, x_sample, cache_k, cache_v, state_gla, page_table, norm1_gain, w_in, w_a2, b_a, gla_gain,
           lambda_q1, lambda_k1, lambda_q2, lambda_k2, diff_gain, w_o, norm2_gain, peer_wq, peer_keys1,
           peer_keys2, peer_u, peer_v, final_gain):
    depth = w_in.shape[0]
    assert depth == 1
    B, T, D = x_prompt.shape
    DB, TS, _ = x_sample.shape
    n_p, n_s = B * T, DB * TS
    n = n_p + n_s
    gla_dv = D // (2 * GLA_HEADS)
    gla_dk = gla_dv // 2
    diff_dv = D // (2 * DIFF_HEADS)
    diff_dk = diff_dv // 2
    gqk_w = GLA_HEADS * gla_dk
    gv_w = GLA_HEADS * gla_dv
    dqk_w = DIFF_HEADS * 2 * diff_dk
    dv_w = DIFF_HEADS * diff_dv
    g_w = 2 * gqk_w + 2 * gv_w
    d_w = 2 * dqk_w + dv_w
    tl = _tiles(n, n_s)
    l = 0
    lam_init = 0.8 - 0.6 * math.exp(-0.3 * l)

    x = jnp.concatenate([x_prompt.reshape(n_p, D), x_sample.reshape(n_s, D)], axis=0)

    w = w_in[l]
    w_main = jnp.concatenate([w[:, :g_w], w[:, g_w + GLA_LOWRANK:]], axis=1).astype(BF16)
    w_ga = jnp.pad(w[:, g_w:g_w + GLA_LOWRANK], ((0, 0), (0, LANES - GLA_LOWRANK))).astype(BF16)
    wa2p = jnp.pad(w_a2[l], ((0, LANES - GLA_LOWRANK), (0, 0)))
    f32 = F32
    lam = (jnp.exp(jnp.sum(lambda_q1[l].astype(f32) * lambda_k1[l].astype(f32)))
           - jnp.exp(jnp.sum(lambda_q2[l].astype(f32) * lambda_k2[l].astype(f32))) + lam_init).reshape(1)
    slopes = _alibi_slopes(DIFF_HEADS)

    xn = rms_norm(x, norm1_gain[l], tl["rms_tm"])
    proj = matmul(xn, w_main, tl["mm_tm"], 1024, name="proj")
    ga = matmul(xn, w_ga, tl["mm_tm"], LANES, name="proj_gate")

    gla_cols = dict(col_q=0, col_k=gqk_w, col_v=2 * gqk_w, col_r=2 * gqk_w + gv_w, dk=gla_dk, dv=gla_dv)
    og_p, s_p = gla(proj, ga, wa2p, b_a[l].reshape(1, -1), gla_gain[l].reshape(1, -1), None,
                    nbatch=B, seqlen=T, chunk=64, valid=64, row0=0, out_dtype=BF16, name="gla_prompt",
                    **gla_cols)
    pad_t = GLA_SUB
    proj_s = jnp.pad(proj[n_p:, :g_w].reshape(DB, TS, g_w), ((0, 0), (0, pad_t - TS), (0, 0))).reshape(DB * pad_t, g_w)
    ga_s = jnp.pad(ga[n_p:].reshape(DB, TS, LANES), ((0, 0), (0, pad_t - TS), (0, 0))).reshape(DB * pad_t, LANES)
    og_s, s_s = gla(proj_s, ga_s, wa2p, b_a[l].reshape(1, -1), gla_gain[l].reshape(1, -1), state_gla[l],
                    nbatch=DB, seqlen=pad_t, chunk=pad_t, valid=TS, row0=0, out_dtype=F32, name="gla_sample",
                    **gla_cols)
    og_s = og_s.reshape(DB, pad_t, gv_w)[:, :TS].reshape(n_s, gv_w)

    out_scale = 1.0 - lam_init
    dgain = diff_gain[l].reshape(1, -1)
    tq = min(512, T)
    od_p = attn_prompt(proj, slopes, lam, dgain, nbatch=B, seqlen=T, tq=tq, tk=tq,
                       col_q=g_w, col_k=g_w + dqk_w, col_v=g_w + 2 * dqk_w, dk=diff_dk, dv=diff_dv,
                       out_scale=out_scale)
    n_pool = cache_k.shape[1]
    ck = cache_k[l].reshape(n_pool, PAGE_SIZE, dqk_w)
    cv = cache_v[l].reshape(n_pool, PAGE_SIZE, dv_w)
    od_s = attn_sample(proj, ck, cv, page_table, slopes, lam, dgain, nbatch=DB, tnew=TS, row0=n_p, pg=4,
                       colblk_q=g_w // dqk_w, colblk_k=g_w // dqk_w + 1, colblk_v=g_w // dqk_w + 2,
                       dk=diff_dk, dv=diff_dv, out_scale=out_scale)

    mix_in = jnp.concatenate([jnp.concatenate([og_p, od_p], axis=1),
                              jnp.concatenate([og_s.astype(BF16), od_s.astype(BF16)], axis=1)], axis=0)
    h = matmul(mix_in, w_o[l].astype(BF16), tl["mm_tm"], 1024, res=x, name="out_proj")

    xn2 = rms_norm(h, norm2_gain[l], tl["rms_tm"])
    qt = matmul_nt(peer_wq[l].T.astype(BF16), xn2, PEER_HEADS * PEER_DQ, tl["peer_tt"], name="peer_query")
    a1t, s2t, tht = peer_route(qt, peer_keys1[l], peer_keys2[l], tl["route_tn"])
    act_t = peer_act(xn2, peer_u[l].astype(BF16), a1t, s2t, tht, tt=tl["peer_tt"], te=1024,
                     tchunk=min(256, tl["peer_tt"]))
    peer_t = matmul_kacc(peer_v[l].T.astype(BF16), act_t, 1024, tl["peer_tt"], 2048, name="peer_out")

    y_p = final_norm(h, peer_t, final_gain, row0=0, nrows=n_p, tm=tl["final_tm"])
    y_s = final_norm(h, peer_t, final_gain, row0=n_p, nrows=n_s, tm=tl["final_tm"])

    k_p = proj[:n_p, g_w + dqk_w:g_w + 2 * dqk_w].reshape(1, B, T, DIFF_HEADS, 2, diff_dk)
    v_p = proj[:n_p, g_w + 2 * dqk_w:].reshape(1, B, T, DIFF_HEADS, diff_dv)
    k_s = proj[n_p:, g_w + dqk_w:g_w + 2 * dqk_w].reshape(1, DB, TS, DIFF_HEADS, 2, diff_dk)
    v_s = proj[n_p:, g_w + 2 * dqk_w:].reshape(1, DB, TS, DIFF_HEADS, diff_dv)
    return (y_p.reshape(B, T, D), y_s.reshape(DB, TS, D), k_p, v_p, s_p[None], k_s, v_s, s_s[None])
```

```python
import functools
import math

import jax
import jax.numpy as jnp
import numpy as np
from jax import lax
from jax.experimental import pallas as pl
from jax.experimental.pallas import tpu as pltpu

F32 = jnp.float32
BF16 = jnp.bfloat16

GLA_HEADS = 4
GLA_LOWRANK = 16
GLA_GATE_TEMP = 16.0
DIFF_HEADS = 8
PEER_HEADS = 8
PEER_NKEYS = 128
PEER_DQ = 128
PEER_TOPK = 16
PAGE_SIZE = 128
RMS_EPS = 1e-6
LANES = 128
SUBLANES = 8
VMEM_LIMIT = 56 * 1024 * 1024
LOG2E = math.log2(math.e)

NT_DIMS = (((1,), (1,)), ((), ()))
TN_DIMS = (((0,), (0,)), ((), ()))
NEG_INF = float("-inf")


def _cparams(sem):
    return pltpu.CompilerParams(dimension_semantics=sem, vmem_limit_bytes=VMEM_LIMIT)


def _rms_kernel(x_ref, g_ref, o_ref):
    x = x_ref[...]
    ms = jnp.mean(x * x, axis=-1, keepdims=True)
    o_ref[...] = (x * lax.rsqrt(ms + RMS_EPS) * g_ref[...]).astype(o_ref.dtype)


def rms_norm(x, g, tm, out_dtype=BF16):
    n, d = x.shape
    return pl.pallas_call(
        _rms_kernel,
        grid=(n // tm,),
        in_specs=[pl.BlockSpec((tm, d), lambda i: (i, 0)),
                  pl.BlockSpec((1, d), lambda i: (0, 0))],
        out_specs=pl.BlockSpec((tm, d), lambda i: (i, 0)),
        out_shape=jax.ShapeDtypeStruct((n, d), out_dtype),
        compiler_params=_cparams(("parallel",)),
        name="rms_norm",
    )(x, g.reshape(1, d))


def _mm_kernel(x_ref, w_ref, o_ref):
    o_ref[...] = jnp.dot(x_ref[...], w_ref[...], preferred_element_type=F32)


def _mm_res_kernel(x_ref, w_ref, r_ref, o_ref):
    o_ref[...] = r_ref[...] + jnp.dot(x_ref[...], w_ref[...], preferred_element_type=F32)


def matmul(x, w, tm, tn, res=None, name="matmul"):
    n, k = x.shape
    m = w.shape[1]
    in_specs = [pl.BlockSpec((tm, k), lambda i, j: (i, 0)),
                pl.BlockSpec((k, tn), lambda i, j: (0, j))]
    args = [x, w]
    kern = _mm_kernel
    if res is not None:
        in_specs.append(pl.BlockSpec((tm, tn), lambda i, j: (i, j)))
        args.append(res)
        kern = _mm_res_kernel
    return pl.pallas_call(
        kern,
        grid=(n // tm, m // tn),
        in_specs=in_specs,
        out_specs=pl.BlockSpec((tm, tn), lambda i, j: (i, j)),
        out_shape=jax.ShapeDtypeStruct((n, m), F32),
        compiler_params=_cparams(("parallel", "parallel")),
        name=name,
    )(*args)


def _mm_nt_kernel(a_ref, b_ref, o_ref):
    o_ref[...] = lax.dot_general(a_ref[...], b_ref[...], NT_DIMS, preferred_element_type=F32)


def matmul_nt(a, b, tm, tn, name="matmul_nt"):
    m, k = a.shape
    n = b.shape[0]
    return pl.pallas_call(
        _mm_nt_kernel,
        grid=(n // tn, m // tm),
        in_specs=[pl.BlockSpec((tm, k), lambda j, i: (i, 0)),
                  pl.BlockSpec((tn, k), lambda j, i: (j, 0))],
        out_specs=pl.BlockSpec((tm, tn), lambda j, i: (i, j)),
        out_shape=jax.ShapeDtypeStruct((m, n), F32),
        compiler_params=_cparams(("parallel", "parallel")),
        name=name,
    )(a, b)


def _mm_acc_kernel(a_ref, b_ref, o_ref):
    @pl.when(pl.program_id(2) == 0)
    def _():
        o_ref[...] = jnp.zeros_like(o_ref)

    o_ref[...] += jnp.dot(a_ref[...], b_ref[...], preferred_element_type=F32)


def matmul_kacc(a, b, tm, tn, tk, name="matmul_kacc"):
    m, k = a.shape
    n = b.shape[1]
    return pl.pallas_call(
        _mm_acc_kernel,
        grid=(m // tm, n // tn, k // tk),
        in_specs=[pl.BlockSpec((tm, tk), lambda i, j, kk: (i, kk)),
                  pl.BlockSpec((tk, tn), lambda i, j, kk: (kk, j))],
        out_specs=pl.BlockSpec((tm, tn), lambda i, j, kk: (i, j)),
        out_shape=jax.ShapeDtypeStruct((m, n), F32),
        compiler_params=_cparams(("parallel", "parallel", "arbitrary")),
        name=name,
    )(a, b)


GLA_SUB = 16


def _log_sigmoid(x):
    return jnp.minimum(x, 0.0) - jnp.log1p(jnp.exp(-jnp.abs(x)))


def _gla_kernel(*refs, chunk, valid, has_s0, dk, dv):
    if has_s0:
        (q_ref, k_ref, v_ref, r_ref, ga_ref, wa_ref, ba_ref, gain_ref, s0_ref, o_ref, s_ref, st_scr) = refs
    else:
        (q_ref, k_ref, v_ref, r_ref, ga_ref, wa_ref, ba_ref, gain_ref, o_ref, s_ref, st_scr) = refs
        s0_ref = None
    c = pl.program_id(2)
    nc = pl.num_programs(2)

    @pl.when(c == 0)
    def _():
        if has_s0:
            st_scr[...] = s0_ref[...].T
        else:
            st_scr[...] = jnp.zeros_like(st_scr)

    def rows(ref):
        x = ref[...]
        if valid < chunk:
            x = jnp.concatenate([x, jnp.zeros((chunk - valid, x.shape[1]), x.dtype)], axis=0)
        return x

    q = rows(q_ref) * (dk ** -0.5)
    k = rows(k_ref)
    v = rows(v_ref)
    pre = jnp.dot(rows(ga_ref), wa_ref[...], preferred_element_type=F32,
                  precision=lax.Precision.HIGHEST) + ba_ref[...]
    g = _log_sigmoid(pre) * (1.0 / GLA_GATE_TEMP)
    if valid < chunk:
        g = jnp.where(lax.broadcasted_iota(jnp.int32, (chunk, 1), 0) < valid, g, 0.0)
    ri = lax.broadcasted_iota(jnp.int32, (chunk, chunk), 0)
    ci = lax.broadcasted_iota(jnp.int32, (chunk, chunk), 1)
    tril = (ci <= ri).astype(F32)
    G = jnp.dot(tril, g, preferred_element_type=F32, precision=lax.Precision.HIGHEST)
    g_last = G[chunk - 1:chunk, :]

    st = st_scr[...]
    qg = (q * jnp.exp(G)).astype(BF16)
    o = lax.dot_general(qg, st.astype(BF16), NT_DIMS, preferred_element_type=F32)

    sub = min(GLA_SUB, chunk)
    lane = lax.broadcasted_iota(jnp.int32, (sub, chunk), 1)
    srow = lax.broadcasted_iota(jnp.int32, (sub, chunk), 0)
    a_rows = []
    for i in range(chunk // sub):
        lo = i * sub
        Gi = G[lo:lo + sub, :]
        qi = q[lo:lo + sub, :]
        if i > 0:
            ref_i = G[lo - 1:lo, :]
            qs = (qi * jnp.exp(Gi - ref_i)).astype(BF16)
            ks = (k * jnp.exp(jnp.minimum(ref_i - G, 0.0))).astype(BF16)
            a_i = lax.dot_general(qs, ks, NT_DIMS, preferred_element_type=F32)
            a_i = jnp.where(lane < lo, a_i, 0.0)
        else:
            a_i = jnp.zeros((sub, chunk), F32)
        for s in range(sub):
            w = qi * jnp.exp(jnp.minimum(Gi - G[lo + s:lo + s + 1, :], 0.0)) * k[lo + s:lo + s + 1, :]
            col = jnp.sum(w, axis=1, keepdims=True)
            a_i = jnp.where(lane == lo + s, col, a_i)
        a_rows.append(jnp.where(lane <= lo + srow, a_i, 0.0))
    a_mat = a_rows[0] if len(a_rows) == 1 else jnp.concatenate(a_rows, axis=0)
    o = o + jnp.dot(a_mat.astype(BF16), v.astype(BF16), preferred_element_type=F32)

    kd = (k * jnp.exp(g_last - G)).astype(BF16)
    upd = lax.dot_general(v.astype(BF16), kd, TN_DIMS, preferred_element_type=F32)
    st_new = st * jnp.exp(g_last) + upd
    st_scr[...] = st_new

    o = o[:valid, :]
    ms = jnp.mean(o * o, axis=-1, keepdims=True)
    r = r_ref[...]
    o = o * lax.rsqrt(ms + RMS_EPS) * gain_ref[...] * (r * jax.nn.sigmoid(r))
    o_ref[...] = o.astype(o_ref.dtype)

    @pl.when(c == nc - 1)
    def _():
        s_ref[...] = st_new.T


def gla(p, ga, wa2p, b_a, gain, s0, *, nbatch, seqlen, chunk, valid, row0, col_q, col_k, col_v, col_r,
        dk, dv, out_dtype, name):
    nc = seqlen // valid
    rb0 = row0 // valid
    h_n = GLA_HEADS
    has_s0 = s0 is not None

    def rowblk(b, h, c):
        return rb0 + b * nc + c

    in_specs = [
        pl.BlockSpec((valid, dk), lambda b, h, c: (rowblk(b, h, c), col_q // dk + h)),
        pl.BlockSpec((valid, dk), lambda b, h, c: (rowblk(b, h, c), col_k // dk + h)),
        pl.BlockSpec((valid, dv), lambda b, h, c: (rowblk(b, h, c), col_v // dv + h)),
        pl.BlockSpec((valid, dv), lambda b, h, c: (rowblk(b, h, c), col_r // dv + h)),
        pl.BlockSpec((valid, LANES), lambda b, h, c: (rowblk(b, h, c), 0)),
        pl.BlockSpec((LANES, dk), lambda b, h, c: (0, h)),
        pl.BlockSpec((1, dk), lambda b, h, c: (0, h)),
        pl.BlockSpec((1, dv), lambda b, h, c: (0, h)),
    ]
    args = [p, p, p, p, ga, wa2p, b_a, gain]
    if has_s0:
        in_specs.append(pl.BlockSpec((None, None, dk, dv), lambda b, h, c: (b, h, 0, 0)))
        args.append(s0)
    kern = functools.partial(_gla_kernel, chunk=chunk, valid=valid, has_s0=has_s0, dk=dk, dv=dv)
    return pl.pallas_call(
        kern,
        grid=(nbatch, h_n, nc),
        in_specs=in_specs,
        out_specs=[pl.BlockSpec((valid, dv), lambda b, h, c: (b * nc + c, h)),
                   pl.BlockSpec((None, None, dk, dv), lambda b, h, c: (b, h, 0, 0))],
        out_shape=[jax.ShapeDtypeStruct((nbatch * seqlen, h_n * dv), out_dtype),
                   jax.ShapeDtypeStruct((nbatch, h_n, dk, dv), F32)],
        scratch_shapes=[pltpu.VMEM((dv, dk), F32)],
        compiler_params=_cparams(("parallel", "parallel", "arbitrary")),
        name=name,
    )(*args)


def _attn_prompt_kernel(slopes_ref, lam_ref, q_ref, k_ref, v_ref, gain_ref, o_ref,
                        m_scr, l_scr, acc_scr, *, tq, tk, dk, dv, out_scale):
    h = pl.program_id(1)
    qi = pl.program_id(2)
    ki = pl.program_id(3)

    @pl.when(ki == 0)
    def _():
        m_scr[...] = jnp.full_like(m_scr, NEG_INF)
        l_scr[...] = jnp.zeros_like(l_scr)
        acc_scr[...] = jnp.zeros_like(acc_scr)

    def step(masked):
        krel = (ki * tk - qi * tq + lax.broadcasted_iota(jnp.int32, (1, tk), 1)).astype(F32)
        kbias = (slopes_ref[h] * LOG2E) * krel
        vb = v_ref[...].astype(BF16)
        if masked:
            keep = (lax.broadcasted_iota(jnp.int32, (tq, tk), 0)
                    >= lax.broadcasted_iota(jnp.int32, (tq, tk), 1))
        for c in range(2):
            qc = (q_ref[:, c * dk:(c + 1) * dk] * (dk ** -0.5 * LOG2E)).astype(BF16)
            kc = k_ref[:, c * dk:(c + 1) * dk].astype(BF16)
            s = lax.dot_general(qc, kc, NT_DIMS, preferred_element_type=F32) + kbias
            if masked:
                s = jnp.where(keep, s, NEG_INF)
            m_prev = m_scr[c]
            m_new = jnp.maximum(m_prev, jnp.max(s, axis=-1, keepdims=True))
            alpha = jnp.exp2(m_prev - m_new)
            p = jnp.exp2(s - m_new)
            l_scr[c] = alpha * l_scr[c] + jnp.sum(p, axis=-1, keepdims=True)
            acc_scr[c] = alpha * acc_scr[c] + jnp.dot(p.astype(BF16), vb, preferred_element_type=F32)
            m_scr[c] = m_new

    @pl.when(ki < qi)
    def _():
        step(False)

    @pl.when(ki == qi)
    def _():
        step(True)
        lam = lam_ref[0]
        o = acc_scr[0] / l_scr[0] - lam * (acc_scr[1] / l_scr[1])
        ms = jnp.mean(o * o, axis=-1, keepdims=True)
        o = o * lax.rsqrt(ms + RMS_EPS) * gain_ref[...] * out_scale
        o_ref[...] = o.astype(o_ref.dtype)


def attn_prompt(p, slopes, lam, gain, *, nbatch, seqlen, tq, tk, col_q, col_k, col_v, dk, dv, out_scale):
    nq = seqlen // tq
    nk = seqlen // tk
    assert tq == tk
    kern = functools.partial(_attn_prompt_kernel, tq=tq, tk=tk, dk=dk, dv=dv, out_scale=out_scale)
    smem = pl.BlockSpec(memory_space=pltpu.SMEM)
    return pl.pallas_call(
        kern,
        grid=(nbatch, DIFF_HEADS, nq, nk),
        in_specs=[smem, smem,
                  pl.BlockSpec((tq, 2 * dk), lambda b, h, i, j: (b * nq + i, col_q // (2 * dk) + h)),
                  pl.BlockSpec((tk, 2 * dk), lambda b, h, i, j: (b * nk + jnp.minimum(i, j), col_k // (2 * dk) + h)),
                  pl.BlockSpec((tk, dv), lambda b, h, i, j: (b * nk + jnp.minimum(i, j), col_v // dv + h)),
                  pl.BlockSpec((1, dv), lambda b, h, i, j: (0, 0))],
        out_specs=pl.BlockSpec((tq, dv), lambda b, h, i, j: (b * nq + i, h)),
        out_shape=jax.ShapeDtypeStruct((nbatch * seqlen, DIFF_HEADS * dv), BF16),
        scratch_shapes=[pltpu.VMEM((2, tq, 1), F32), pltpu.VMEM((2, tq, 1), F32),
                        pltpu.VMEM((2, tq, dv), F32)],
        compiler_params=_cparams(("parallel", "parallel", "parallel", "arbitrary")),
        name="attn_prompt",
    )(slopes, lam, p, p, p, gain)


def _attn_sample_kernel(pt_ref, lam_ref, q_ref, kn_ref, vn_ref, gain_ref, c0n_ref, c0s_ref, scol_ref, *rest,
                        pg, tnew, past_len, dk, dv, out_scale):
    k_refs = rest[:pg]
    v_refs = rest[pg:2 * pg]
    o_ref = rest[2 * pg]
    qt_scr, m_scr, l_scr, acc_scr = rest[2 * pg + 1:]
    j = pl.program_id(1)
    nj = pl.num_programs(1)
    nh = DIFF_HEADS
    half = nh * tnew
    page_lanes = PAGE_SIZE * nh

    @pl.when(j == 0)
    def _():
        s_parts = []
        for c in range(2):
            cols = [slice((2 * h + c) * dk, (2 * h + c + 1) * dk) for h in range(nh)]
            qt = jnp.concatenate([q_ref[:, cs] for cs in cols], axis=0) * (dk ** -0.5 * LOG2E)
            qt_scr[c] = qt.astype(BF16)
            kn = jnp.concatenate([kn_ref[:, cs] for cs in cols], axis=0)
            s_parts.append(lax.dot_general(qt, kn, NT_DIMS, preferred_element_type=F32))
        s = jnp.concatenate(s_parts, axis=0) - c0n_ref[...]
        m = jnp.max(s, axis=-1, keepdims=True)
        p = jnp.exp2(s - m)
        vn = jnp.concatenate([vn_ref[:, h * dv:(h + 1) * dv] for h in range(nh)], axis=0)
        m_scr[...] = m
        l_scr[...] = jnp.sum(p, axis=-1, keepdims=True)
        acc_scr[...] = jnp.dot(p, vn, preferred_element_type=F32)

    s_parts = []
    for c in range(2):
        qt = qt_scr[c]
        s_parts.append(jnp.concatenate(
            [lax.dot_general(qt, k_refs[i][pl.ds(c, page_lanes, stride=2), :].astype(BF16), NT_DIMS,
                             preferred_element_type=F32) for i in range(pg)], axis=-1))
    s = jnp.concatenate(s_parts, axis=0) - c0s_ref[...]
    addcol = scol_ref[...] * (j * (pg * PAGE_SIZE) - past_len).astype(F32)
    m_prev = m_scr[...]
    m_new = jnp.maximum(m_prev, jnp.max(s, axis=-1, keepdims=True) + addcol)
    alpha = jnp.exp2(m_prev - m_new)
    p = jnp.exp2(s + (addcol - m_new))
    l_scr[...] = alpha * l_scr[...] + jnp.sum(p, axis=-1, keepdims=True)
    pb = p.astype(BF16)
    acc = alpha * acc_scr[...]
    for i in range(pg):
        vi = v_refs[i][...].reshape(page_lanes, dv).astype(BF16)
        acc = acc + jnp.dot(pb[:, i * page_lanes:(i + 1) * page_lanes], vi, preferred_element_type=F32)
    acc_scr[...] = acc
    m_scr[...] = m_new

    @pl.when(j == nj - 1)
    def _():
        lam = lam_ref[0]
        on = acc / l_scr[...]
        for h in range(nh):
            o = on[h * tnew:(h + 1) * tnew, :] - lam * on[half + h * tnew:half + (h + 1) * tnew, :]
            ms = jnp.mean(o * o, axis=-1, keepdims=True)
            o_ref[:, h * dv:(h + 1) * dv] = o * lax.rsqrt(ms + RMS_EPS) * gain_ref[...] * out_scale


def _sample_bias_tiles(slopes, tnew, pg):
    nh = DIFF_HEADS
    r = np.arange(2 * nh * tnew)
    rh, rq = (r // tnew) % nh, r % tnew
    sl = np.asarray(slopes, np.float64)[rh] * LOG2E
    ln = np.arange(nh * tnew)
    lh, lt = ln // tnew, ln % tnew
    ok = (lh[None, :] == rh[:, None]) & (lt[None, :] <= rq[:, None])
    c0n = np.where(ok, sl[:, None] * (rq[:, None] - lt[None, :]), np.inf)
    ls = np.arange(pg * PAGE_SIZE * nh)
    lh, lt = ls % nh, ls // nh
    ok = lh[None, :] == rh[:, None]
    c0s = np.where(ok, sl[:, None] * (rq[:, None] - lt[None, :]), np.inf)
    return (jnp.asarray(c0n, F32), jnp.asarray(c0s, F32), jnp.asarray(sl[:, None], F32))


def attn_sample(p, cache_k, cache_v, page_table, slopes, lam, gain, *, nbatch, tnew, row0, pg,
                colblk_q, colblk_k, colblk_v, dk, dv, out_scale):
    n_pages = page_table.shape[1]
    nj = n_pages // pg
    nh = DIFF_HEADS
    width = nh * dv
    rb0 = row0 // tnew
    rows = 2 * nh * tnew
    c0n, c0s, scol = _sample_bias_tiles(slopes, tnew, pg)
    kern = functools.partial(_attn_sample_kernel, pg=pg, tnew=tnew, past_len=n_pages * PAGE_SIZE,
                             dk=dk, dv=dv, out_scale=out_scale)
    smem = pl.BlockSpec(memory_space=pltpu.SMEM)

    def kpage_spec(i):
        return pl.BlockSpec((None, PAGE_SIZE * nh * 2, dk), lambda b, j, pt: (pt[b, j * pg + i], 0, 0))

    def vpage_spec(i):
        return pl.BlockSpec((None, PAGE_SIZE, nh, dv), lambda b, j, pt: (pt[b, j * pg + i], 0, 0, 0))

    in_specs = [smem,
                pl.BlockSpec((tnew, width), lambda b, j, pt: (rb0 + b, colblk_q)),
                pl.BlockSpec((tnew, width), lambda b, j, pt: (rb0 + b, colblk_k)),
                pl.BlockSpec((tnew, width), lambda b, j, pt: (rb0 + b, colblk_v)),
                pl.BlockSpec((1, dv), lambda b, j, pt: (0, 0)),
                pl.BlockSpec(c0n.shape, lambda b, j, pt: (0, 0)),
                pl.BlockSpec(c0s.shape, lambda b, j, pt: (0, 0)),
                pl.BlockSpec((rows, 1), lambda b, j, pt: (0, 0))]
    in_specs += [kpage_spec(i) for i in range(pg)] + [vpage_spec(i) for i in range(pg)]
    grid_spec = pltpu.PrefetchScalarGridSpec(
        num_scalar_prefetch=1,
        grid=(nbatch, nj),
        in_specs=in_specs,
        out_specs=pl.BlockSpec((tnew, width), lambda b, j, pt: (b, 0)),
        scratch_shapes=[pltpu.VMEM((2, nh * tnew, dk), BF16),
                        pltpu.VMEM((rows, 1), F32),
                        pltpu.VMEM((rows, 1), F32),
                        pltpu.VMEM((rows, dv), F32)],
    )
    return pl.pallas_call(
        kern,
        grid_spec=grid_spec,
        out_shape=jax.ShapeDtypeStruct((nbatch * tnew, width), F32),
        compiler_params=_cparams(("parallel", "arbitrary")),
        name="attn_sample",
    )(page_table, lam, p, p, p, gain, c0n, c0s, scol, *([cache_k] * pg), *([cache_v] * pg))


N_TOP = PEER_TOPK + 1


def _top_values(x, n):
    cur = jnp.max(x, axis=0, keepdims=True)
    vals = [cur]
    for _ in range(n - 1):
        cur = jnp.max(jnp.where(x < cur, x, NEG_INF), axis=0, keepdims=True)
        vals.append(cur)
    return vals


_CAND_PAIRS = [(a, b) for a in range(N_TOP) for b in range(N_TOP) if (a + 1) * (b + 1) <= N_TOP]
_CAND_ROWS = -(-len(_CAND_PAIRS) // SUBLANES) * SUBLANES


def _route_kernel(qt_ref, k1_ref, k2_ref, a1_ref, s2_ref, th_ref, cand_scr):
    half = PEER_DQ // 2
    tn = qt_ref.shape[1]
    cand_scr[...] = jnp.full_like(cand_scr, NEG_INF)
    for h in range(PEER_HEADS):
        q1 = qt_ref[h * PEER_DQ:h * PEER_DQ + half, :]
        q2 = qt_ref[h * PEER_DQ + half:(h + 1) * PEER_DQ, :]
        s1 = jnp.dot(k1_ref[...], q1, preferred_element_type=F32, precision=lax.Precision.HIGHEST)
        s2 = jnp.dot(k2_ref[...], q2, preferred_element_type=F32, precision=lax.Precision.HIGHEST)
        t1 = _top_values(s1, N_TOP)
        t2 = _top_values(s2, N_TOP)
        for i, (a, b) in enumerate(_CAND_PAIRS):
            cand_scr[i:i + 1, :] = t1[a] + t2[b]
        c = _top_values(cand_scr[...], N_TOP)
        m = c[0]
        z = jnp.zeros_like(m)
        for i in range(PEER_TOPK):
            z = z + jnp.exp(c[i] - m)
        shift = m + jnp.log(z) + math.log(2.0)
        thr = (0.5 * (c[PEER_TOPK - 1] + c[PEER_TOPK]) - shift) * LOG2E
        a1 = (s1 - shift) * LOG2E
        s2 = s2 * LOG2E
        for cb in range(tn // LANES):
            a1_ref[h, cb] = a1[:, cb * LANES:(cb + 1) * LANES]
            s2_ref[h, cb] = s2[:, cb * LANES:(cb + 1) * LANES]
            th_ref[cb, h:h + 1, :] = thr[:, cb * LANES:(cb + 1) * LANES]


def peer_route(qt, keys1, keys2, tn):
    n = qt.shape[1]
    nk = PEER_NKEYS
    nb = tn // LANES
    tile4 = pl.BlockSpec((PEER_HEADS, nb, nk, LANES), lambda i: (0, i, 0, 0))
    shape4 = jax.ShapeDtypeStruct((PEER_HEADS, n // LANES, nk, LANES), F32)
    return pl.pallas_call(
        _route_kernel,
        grid=(n // tn,),
        in_specs=[pl.BlockSpec((PEER_HEADS * PEER_DQ, tn), lambda i: (0, i)),
                  pl.BlockSpec((nk, PEER_DQ // 2), lambda i: (0, 0)),
                  pl.BlockSpec((nk, PEER_DQ // 2), lambda i: (0, 0))],
        out_specs=[tile4, tile4, pl.BlockSpec((nb, PEER_HEADS, LANES), lambda i: (i, 0, 0))],
        out_shape=[shape4, shape4, jax.ShapeDtypeStruct((n // LANES, PEER_HEADS, LANES), F32)],
        scratch_shapes=[pltpu.VMEM((_CAND_ROWS, tn), F32)],
        compiler_params=_cparams(("parallel",)),
        name="peer_route",
    )(qt, keys1, keys2)


_GELU_C = math.sqrt(2.0 / math.pi)


def _peer_act_kernel(x_ref, u_ref, s2_ref, a1_ref, th_ref, o_ref):
    te, tt = o_ref.shape
    nk = PEER_NKEYS
    tcol = min(tt, 2 * LANES)
    for c2 in range(tt // tcol):
        pre = lax.dot_general(u_ref[...], x_ref[c2 * tcol:(c2 + 1) * tcol, :], NT_DIMS,
                              preferred_element_type=F32)
        for r in range(te // nk):
            for cl in range(tcol // LANES):
                cb = c2 * (tcol // LANES) + cl
                w = jnp.zeros((nk, LANES), F32)
                for h in range(PEER_HEADS):
                    a1 = jnp.tile(jnp.broadcast_to(a1_ref[h, cb, r:r + 1, :], (SUBLANES, LANES)),
                                  (nk // SUBLANES, 1))
                    th = jnp.tile(jnp.broadcast_to(th_ref[cb, h:h + 1, :], (SUBLANES, LANES)),
                                  (nk // SUBLANES, 1))
                    d = s2_ref[h, cb] + a1
                    w = w + jnp.where(d > th, jnp.exp2(d), 0.0)
                x = pre[r * nk:(r + 1) * nk, cl * LANES:(cl + 1) * LANES]
                g = x * (1.0 + jnp.tanh(_GELU_C * (x + 0.044715 * (x * x * x))))
                o_ref[r * nk:(r + 1) * nk, cb * LANES:(cb + 1) * LANES] = (g * w).astype(o_ref.dtype)


def peer_act(xn, u, a1t, s2t, tht, *, tt, te):
    n, d = xn.shape
    ne = u.shape[0]
    rows = te // PEER_NKEYS
    nb = tt // LANES
    return pl.pallas_call(
        _peer_act_kernel,
        grid=(n // tt, ne // te),
        in_specs=[pl.BlockSpec((tt, d), lambda t, e: (t, 0)),
                  pl.BlockSpec((te, d), lambda t, e: (e, 0)),
                  pl.BlockSpec((PEER_HEADS, nb, PEER_NKEYS, LANES), lambda t, e: (0, t, 0, 0)),
                  pl.BlockSpec((PEER_HEADS, nb, rows, LANES), lambda t, e: (0, t, e, 0)),
                  pl.BlockSpec((nb, PEER_HEADS, LANES), lambda t, e: (t, 0, 0))],
        out_specs=pl.BlockSpec((te, tt), lambda t, e: (e, t)),
        out_shape=jax.ShapeDtypeStruct((ne, n), BF16),
        compiler_params=_cparams(("parallel", "arbitrary")),
        name="peer_act",
    )(xn, u, s2t, a1t, tht)


def _final_kernel(h_ref, pt_ref, g_ref, o_ref):
    x = h_ref[...] + pt_ref[...].T
    ms = jnp.mean(x * x, axis=-1, keepdims=True)
    o_ref[...] = x * lax.rsqrt(ms + RMS_EPS) * g_ref[...]


def final_norm(h, peer_t, g, *, row0, nrows, tm):
    d = h.shape[1]
    rb0 = row0 // tm
    return pl.pallas_call(
        _final_kernel,
        grid=(nrows // tm,),
        in_specs=[pl.BlockSpec((tm, d), lambda i: (rb0 + i, 0)),
                  pl.BlockSpec((d, tm), lambda i: (0, rb0 + i)),
                  pl.BlockSpec((1, d), lambda i: (0, 0))],
        out_specs=pl.BlockSpec((tm, d), lambda i: (i, 0)),
        out_shape=jax.ShapeDtypeStruct((nrows, d), F32),
        compiler_params=_cparams(("parallel",)),
        name="final_norm",
    )(h, peer_t, g.reshape(1, d))


def _tiles(n, n_s):
    def pick(cands, m=n):
        for c in cands:
            if m % c == 0:
                return c
        raise ValueError(f"no tile for {m}")
    return dict(
        rms_tm=pick((352, 256, 128, 64, 32, 16)),
        mm_tm=pick((768, 512, 256, 128, 64, 32, 16)),
        peer_tt=pick((768, 512, 256, 128)),
        route_tn=pick((256, 128)),
        final_tm=pick((256, 128), math.gcd(n, n_s)),
    )


def _alibi_slopes(nheads):
    return [2.0 ** (-8.0 * (i + 1) / nheads) for i in range(nheads)]


def kernel(x_prompt, x_sample, cache_k, cache_v, state_gla, page_table, norm1_gain, w_in, w_a2, b_a, gla_gain,
           lambda_q1, lambda_k1, lambda_q2, lambda_k2, diff_gain, w_o, norm2_gain, peer_wq, peer_keys1,
           peer_keys2, peer_u, peer_v, final_gain):
    depth = w_in.shape[0]
    assert depth == 1
    B, T, D = x_prompt.shape
    DB, TS, _ = x_sample.shape
    n_p, n_s = B * T, DB * TS
    n = n_p + n_s
    gla_dv = D // (2 * GLA_HEADS)
    gla_dk = gla_dv // 2
    diff_dv = D // (2 * DIFF_HEADS)
    diff_dk = diff_dv // 2
    gqk_w = GLA_HEADS * gla_dk
    gv_w = GLA_HEADS * gla_dv
    dqk_w = DIFF_HEADS * 2 * diff_dk
    dv_w = DIFF_HEADS * diff_dv
    g_w = 2 * gqk_w + 2 * gv_w
    tl = _tiles(n, n_s)
    l = 0
    lam_init = 0.8 - 0.6 * math.exp(-0.3 * l)

    x = jnp.concatenate([x_prompt.reshape(n_p, D), x_sample.reshape(n_s, D)], axis=0)

    w = w_in[l]
    w_main = jnp.concatenate([w[:, :g_w], w[:, g_w + GLA_LOWRANK:]], axis=1).astype(BF16)
    w_ga = jnp.pad(w[:, g_w:g_w + GLA_LOWRANK], ((0, 0), (0, LANES - GLA_LOWRANK))).astype(BF16)
    wa2p = jnp.pad(w_a2[l], ((0, LANES - GLA_LOWRANK), (0, 0)))
    f32 = F32
    lam = (jnp.exp(jnp.sum(lambda_q1[l].astype(f32) * lambda_k1[l].astype(f32)))
           - jnp.exp(jnp.sum(lambda_q2[l].astype(f32) * lambda_k2[l].astype(f32))) + lam_init).reshape(1)
    slopes = _alibi_slopes(DIFF_HEADS)

    xn = rms_norm(x, norm1_gain[l], tl["rms_tm"])
    proj = matmul(xn, w_main, tl["mm_tm"], 1024, name="proj")
    ga = matmul(xn, w_ga, tl["mm_tm"], LANES, name="proj_gate")

    gla_args = (proj, ga, wa2p, b_a[l].reshape(1, -1), gla_gain[l].reshape(1, -1))
    gla_cols = dict(col_q=0, col_k=gqk_w, col_v=2 * gqk_w, col_r=2 * gqk_w + gv_w, dk=gla_dk, dv=gla_dv)
    og_p, s_p = gla(*gla_args, None, nbatch=B, seqlen=T, chunk=64, valid=64, row0=0, out_dtype=BF16,
                    name="gla_prompt", **gla_cols)
    og_s, s_s = gla(*gla_args, state_gla[l], nbatch=DB, seqlen=TS, chunk=GLA_SUB, valid=TS, row0=n_p,
                    out_dtype=F32, name="gla_sample", **gla_cols)

    out_scale = 1.0 - lam_init
    dgain = diff_gain[l].reshape(1, -1)
    tq = min(512, T)
    od_p = attn_prompt(proj, jnp.asarray(slopes, F32), lam, dgain, nbatch=B, seqlen=T, tq=tq, tk=tq,
                       col_q=g_w, col_k=g_w + dqk_w, col_v=g_w + 2 * dqk_w, dk=diff_dk, dv=diff_dv,
                       out_scale=out_scale)
    n_pool = cache_k.shape[1]
    ck = cache_k[l].reshape(n_pool, PAGE_SIZE * DIFF_HEADS * 2, diff_dk)
    od_s = attn_sample(proj, ck, cache_v[l], page_table, slopes, lam, dgain, nbatch=DB, tnew=TS, row0=n_p, pg=4,
                       colblk_q=g_w // dqk_w, colblk_k=g_w // dqk_w + 1, colblk_v=g_w // dqk_w + 2,
                       dk=diff_dk, dv=diff_dv, out_scale=out_scale)

    mix_in = jnp.concatenate([jnp.concatenate([og_p, od_p], axis=1),
                              jnp.concatenate([og_s.astype(BF16), od_s.astype(BF16)], axis=1)], axis=0)
    h = matmul(mix_in, w_o[l].astype(BF16), tl["mm_tm"], 1024, res=x, name="out_proj")

    xn2 = rms_norm(h, norm2_gain[l], tl["rms_tm"])
    qt = matmul_nt(peer_wq[l].T.astype(BF16), xn2, PEER_HEADS * PEER_DQ, tl["peer_tt"], name="peer_query")
    a1t, s2t, tht = peer_route(qt, peer_keys1[l], peer_keys2[l], tl["route_tn"])
    act_t = peer_act(xn2, peer_u[l].astype(BF16), a1t, s2t, tht, tt=tl["peer_tt"], te=1024)
    peer_t = matmul_kacc(peer_v[l].T.astype(BF16), act_t, 1024, tl["peer_tt"], 2048, name="peer_out")

    y_p = final_norm(h, peer_t, final_gain, row0=0, nrows=n_p, tm=tl["final_tm"])
    y_s = final_norm(h, peer_t, final_gain, row0=n_p, nrows=n_s, tm=tl["final_tm"])

    k_p = proj[:n_p, g_w + dqk_w:g_w + 2 * dqk_w].reshape(1, B, T, DIFF_HEADS, 2, diff_dk)
    v_p = proj[:n_p, g_w + 2 * dqk_w:].reshape(1, B, T, DIFF_HEADS, diff_dv)
    k_s = proj[n_p:, g_w + dqk_w:g_w + 2 * dqk_w].reshape(1, DB, TS, DIFF_HEADS, 2, diff_dk)
    v_s = proj[n_p:, g_w + 2 * dqk_w:].reshape(1, DB, TS, DIFF_HEADS, diff_dv)
    return (y_p.reshape(B, T, D), y_s.reshape(DB, TS, D), k_p, v_p, s_p[None], k_s, v_s, s_s[None])
```

```python
import functools
import math

import jax
import jax.numpy as jnp
import numpy as np
from jax import lax
from jax.experimental import pallas as pl
from jax.experimental.pallas import tpu as pltpu

F32 = jnp.float32
BF16 = jnp.bfloat16

GLA_HEADS = 4
GLA_LOWRANK = 16
GLA_GATE_TEMP = 16.0
DIFF_HEADS = 8
PEER_HEADS = 8
PEER_NKEYS = 128
PEER_DQ = 128
PEER_TOPK = 16
PAGE_SIZE = 128
RMS_EPS = 1e-6
LANES = 128
SUBLANES = 8
VMEM_LIMIT = 56 * 1024 * 1024
LOG2E = math.log2(math.e)

NT_DIMS = (((1,), (1,)), ((), ()))
TN_DIMS = (((0,), (0,)), ((), ()))
NEG_INF = float("-inf")


def _cparams(sem):
    return pltpu.CompilerParams(dimension_semantics=sem, vmem_limit_bytes=VMEM_LIMIT)


def _rms_kernel(x_ref, g_ref, *rest):
    o_ref = rest[-1]
    x = x_ref[...]
    ms = jnp.mean(x * x, axis=-1, keepdims=True)
    o_ref[...] = (x * lax.rsqrt(ms + RMS_EPS) * g_ref[...]).astype(o_ref.dtype)


def rms_norm(parts, g, tm, out_dtype=BF16):
    d = parts[0].shape[1]
    total = sum(p.shape[0] for p in parts)
    out = None
    row0 = 0
    for x in parts:
        n = x.shape[0]
        rb0 = row0 // tm
        in_specs = [pl.BlockSpec((tm, d), lambda i: (i, 0)),
                    pl.BlockSpec((1, d), lambda i: (0, 0))]
        args = [x, g.reshape(1, d)]
        aliases = {}
        if out is not None:
            in_specs.append(pl.BlockSpec(memory_space=pl.ANY))
            args.append(out)
            aliases = {2: 0}
        out = pl.pallas_call(
            _rms_kernel,
            grid=(n // tm,),
            in_specs=in_specs,
            out_specs=pl.BlockSpec((tm, d), lambda i, rb0=rb0: (rb0 + i, 0)),
            out_shape=jax.ShapeDtypeStruct((total, d), out_dtype),
            input_output_aliases=aliases,
            compiler_params=_cparams(("parallel",)),
            name="rms_norm",
        )(*args)
        row0 += n
    return out


def _mm_kernel(x_ref, w_ref, o_ref):
    o_ref[...] = jnp.dot(x_ref[...], w_ref[...], preferred_element_type=F32)


def _mm_res_kernel(x_ref, w_ref, r_ref, o_ref):
    o_ref[...] = r_ref[...] + jnp.dot(x_ref[...], w_ref[...], preferred_element_type=F32)


def matmul(x, w, tm, tn, res=None, name="matmul"):
    n, k = x.shape
    m = w.shape[1]
    in_specs = [pl.BlockSpec((tm, k), lambda i, j: (i, 0)),
                pl.BlockSpec((k, tn), lambda i, j: (0, j))]
    args = [x, w]
    kern = _mm_kernel
    if res is not None:
        in_specs.append(pl.BlockSpec((tm, tn), lambda i, j: (i, j)))
        args.append(res)
        kern = _mm_res_kernel
    return pl.pallas_call(
        kern,
        grid=(n // tm, m // tn),
        in_specs=in_specs,
        out_specs=pl.BlockSpec((tm, tn), lambda i, j: (i, j)),
        out_shape=jax.ShapeDtypeStruct((n, m), F32),
        compiler_params=_cparams(("parallel", "parallel")),
        name=name,
    )(*args)


def _mm_nt_kernel(a_ref, b_ref, o_ref):
    o_ref[...] = lax.dot_general(a_ref[...], b_ref[...], NT_DIMS, preferred_element_type=F32)


def matmul_nt(a, b, tm, tn, name="matmul_nt"):
    m, k = a.shape
    n = b.shape[0]
    return pl.pallas_call(
        _mm_nt_kernel,
        grid=(n // tn, m // tm),
        in_specs=[pl.BlockSpec((tm, k), lambda j, i: (i, 0)),
                  pl.BlockSpec((tn, k), lambda j, i: (j, 0))],
        out_specs=pl.BlockSpec((tm, tn), lambda j, i: (i, j)),
        out_shape=jax.ShapeDtypeStruct((m, n), F32),
        compiler_params=_cparams(("parallel", "parallel")),
        name=name,
    )(a, b)


def _mm_acc_kernel(a_ref, b_ref, o_ref):
    @pl.when(pl.program_id(2) == 0)
    def _():
        o_ref[...] = jnp.zeros_like(o_ref)

    tm = o_ref.shape[0]
    rc = min(tm, 1024)
    for r in range(tm // rc):
        o_ref[r * rc:(r + 1) * rc, :] += jnp.dot(a_ref[r * rc:(r + 1) * rc, :], b_ref[...],
                                                 preferred_element_type=F32)


def matmul_kacc(a, b, tm, tn, tk, name="matmul_kacc"):
    m, k = a.shape
    n = b.shape[1]
    return pl.pallas_call(
        _mm_acc_kernel,
        grid=(m // tm, n // tn, k // tk),
        in_specs=[pl.BlockSpec((tm, tk), lambda i, j, kk: (i, kk)),
                  pl.BlockSpec((tk, tn), lambda i, j, kk: (kk, j))],
        out_specs=pl.BlockSpec((tm, tn), lambda i, j, kk: (i, j)),
        out_shape=jax.ShapeDtypeStruct((m, n), F32),
        compiler_params=_cparams(("parallel", "parallel", "arbitrary")),
        name=name,
    )(a, b)


GLA_SUB = 16


def _log_sigmoid(x):
    return jnp.minimum(x, 0.0) - jnp.log1p(jnp.exp(-jnp.abs(x)))


def _gla_head(q, k, v, r, ga, wa, ba, gain, st, *, chunk, valid, dk):
    q = q * (dk ** -0.5)
    pre = jnp.dot(ga, wa, preferred_element_type=F32, precision=lax.Precision.HIGHEST) + ba
    g = _log_sigmoid(pre) * (1.0 / GLA_GATE_TEMP)
    if valid < chunk:
        g = jnp.where(lax.broadcasted_iota(jnp.int32, (chunk, 1), 0) < valid, g, 0.0)
    ri = lax.broadcasted_iota(jnp.int32, (chunk, chunk), 0)
    ci = lax.broadcasted_iota(jnp.int32, (chunk, chunk), 1)
    tril = (ci <= ri).astype(F32)
    G = jnp.dot(tril, g, preferred_element_type=F32, precision=lax.Precision.HIGHEST)
    g_last = G[chunk - 1:chunk, :]

    qg = (q * jnp.exp(G)).astype(BF16)
    o = lax.dot_general(qg, st.astype(BF16), NT_DIMS, preferred_element_type=F32)

    sub = min(GLA_SUB, chunk)
    lane = lax.broadcasted_iota(jnp.int32, (sub, chunk), 1)
    srow = lax.broadcasted_iota(jnp.int32, (sub, chunk), 0)
    a_rows = []
    for i in range(chunk // sub):
        lo = i * sub
        Gi = G[lo:lo + sub, :]
        qi = q[lo:lo + sub, :]
        if i > 0:
            ref_i = G[lo - 1:lo, :]
            qs = (qi * jnp.exp(Gi - ref_i)).astype(BF16)
            ks = (k * jnp.exp(jnp.minimum(ref_i - G, 0.0))).astype(BF16)
            a_i = lax.dot_general(qs, ks, NT_DIMS, preferred_element_type=F32)
            a_i = jnp.where(lane < lo, a_i, 0.0)
        else:
            a_i = jnp.zeros((sub, chunk), F32)
        for s in range(sub):
            w = qi * jnp.exp(jnp.minimum(Gi - G[lo + s:lo + s + 1, :], 0.0)) * k[lo + s:lo + s + 1, :]
            col = jnp.sum(w, axis=1, keepdims=True)
            a_i = jnp.where(lane == lo + s, col, a_i)
        a_rows.append(jnp.where(lane <= lo + srow, a_i, 0.0))
    a_mat = a_rows[0] if len(a_rows) == 1 else jnp.concatenate(a_rows, axis=0)
    o = o + jnp.dot(a_mat.astype(BF16), v.astype(BF16), preferred_element_type=F32)

    kd = (k * jnp.exp(g_last - G)).astype(BF16)
    upd = lax.dot_general(v.astype(BF16), kd, TN_DIMS, preferred_element_type=F32)
    st_new = st * jnp.exp(g_last) + upd

    o = o[:valid, :]
    ms = jnp.mean(o * o, axis=-1, keepdims=True)
    o = o * lax.rsqrt(ms + RMS_EPS) * gain * (r * jax.nn.sigmoid(r))
    return o, st_new


def _gla_kernel(*refs, chunk, valid, has_s0, dk, dv):
    if has_s0:
        (q_ref, k_ref, v_ref, r_ref, ga_ref, wa_ref, ba_ref, gain_ref, s0_ref, o_ref, s_ref, st_scr) = refs
    else:
        (q_ref, k_ref, v_ref, r_ref, ga_ref, wa_ref, ba_ref, gain_ref, o_ref, s_ref, st_scr) = refs
        s0_ref = None
    c = pl.program_id(1)
    nc = pl.num_programs(1)

    @pl.when(c == 0)
    def _():
        for h in range(GLA_HEADS):
            if has_s0:
                st_scr[h] = s0_ref[h].T
            else:
                st_scr[h] = jnp.zeros((dv, dk), F32)

    def rows(x):
        if valid < chunk:
            x = jnp.concatenate([x, jnp.zeros((chunk - valid, x.shape[1]), x.dtype)], axis=0)
        return x

    ga = rows(ga_ref[...])
    for h in range(GLA_HEADS):
        ks = slice(h * dk, (h + 1) * dk)
        vs = slice(h * dv, (h + 1) * dv)
        o, st_new = _gla_head(rows(q_ref[:, ks]), rows(k_ref[:, ks]), rows(v_ref[:, vs]), r_ref[:, vs], ga,
                              wa_ref[:, ks], ba_ref[:, ks], gain_ref[:, vs], st_scr[h],
                              chunk=chunk, valid=valid, dk=dk)
        st_scr[h] = st_new
        o_ref[:, vs] = o.astype(o_ref.dtype)

        @pl.when(c == nc - 1)
        def _():
            s_ref[h] = st_new.T


def gla(p, ga, wa2p, b_a, gain, s0, *, nbatch, seqlen, chunk, valid, row0, out_cols, out_dtype, dk, dv, name):
    nc = seqlen // valid
    rb0 = row0 // valid
    h_n = GLA_HEADS
    qk_w, v_w = h_n * dk, h_n * dv
    assert v_w == 2 * qk_w
    has_s0 = s0 is not None

    def rowblk(b, c):
        return rb0 + b * nc + c

    in_specs = [
        pl.BlockSpec((valid, qk_w), lambda b, c: (rowblk(b, c), 0)),
        pl.BlockSpec((valid, qk_w), lambda b, c: (rowblk(b, c), 1)),
        pl.BlockSpec((valid, v_w), lambda b, c: (rowblk(b, c), 1)),
        pl.BlockSpec((valid, v_w), lambda b, c: (rowblk(b, c), 2)),
        pl.BlockSpec((valid, LANES), lambda b, c: (rowblk(b, c), 0)),
        pl.BlockSpec((LANES, qk_w), lambda b, c: (0, 0)),
        pl.BlockSpec((1, qk_w), lambda b, c: (0, 0)),
        pl.BlockSpec((1, v_w), lambda b, c: (0, 0)),
    ]
    args = [p, p, p, p, ga, wa2p, b_a, gain]
    if has_s0:
        in_specs.append(pl.BlockSpec((None, h_n, dk, dv), lambda b, c: (b, 0, 0, 0)))
        args.append(s0)
    kern = functools.partial(_gla_kernel, chunk=chunk, valid=valid, has_s0=has_s0, dk=dk, dv=dv)
    return pl.pallas_call(
        kern,
        grid=(nbatch, nc),
        in_specs=in_specs,
        out_specs=[pl.BlockSpec((valid, v_w), lambda b, c: (b * nc + c, 0)),
                   pl.BlockSpec((None, h_n, dk, dv), lambda b, c: (b, 0, 0, 0))],
        out_shape=[jax.ShapeDtypeStruct((nbatch * seqlen, out_cols), out_dtype),
                   jax.ShapeDtypeStruct((nbatch, h_n, dk, dv), F32)],
        scratch_shapes=[pltpu.VMEM((h_n, dv, dk), F32)],
        compiler_params=_cparams(("parallel", "arbitrary")),
        name=name,
    )(*args)


def _attn_prompt_kernel(qi_ref, ki_ref, slopes_ref, lam_ref, q_ref, k_ref, v_ref, gain_ref, prev_ref, o_ref,
                        m_scr, l_scr, acc_scr, *, tq, tk, dk, dv, out_scale):
    del prev_ref
    h = pl.program_id(1)
    step_id = pl.program_id(2)
    qi = qi_ref[step_id]
    ki = ki_ref[step_id]

    @pl.when(ki == 0)
    def _():
        m_scr[...] = jnp.full_like(m_scr, NEG_INF)
        l_scr[...] = jnp.zeros_like(l_scr)
        acc_scr[...] = jnp.zeros_like(acc_scr)

    def step(masked):
        krel = (ki * tk - qi * tq + lax.broadcasted_iota(jnp.int32, (1, tk), 1)).astype(F32)
        kbias = (slopes_ref[h] * LOG2E) * krel
        vb = v_ref[...].astype(BF16)
        if masked:
            keep = (lax.broadcasted_iota(jnp.int32, (tq, tk), 0)
                    >= lax.broadcasted_iota(jnp.int32, (tq, tk), 1))
        for c in range(2):
            qc = (q_ref[:, c * dk:(c + 1) * dk] * (dk ** -0.5 * LOG2E)).astype(BF16)
            kc = k_ref[:, c * dk:(c + 1) * dk].astype(BF16)
            s = lax.dot_general(qc, kc, NT_DIMS, preferred_element_type=F32) + kbias
            if masked:
                s = jnp.where(keep, s, NEG_INF)
            m_prev = m_scr[c]
            m_new = jnp.maximum(m_prev, jnp.max(s, axis=-1, keepdims=True))
            alpha = jnp.exp2(m_prev - m_new)
            p = jnp.exp2(s - m_new)
            l_scr[c] = alpha * l_scr[c] + jnp.sum(p, axis=-1, keepdims=True)
            acc_scr[c] = alpha * acc_scr[c] + jnp.dot(p.astype(BF16), vb, preferred_element_type=F32)
            m_scr[c] = m_new

    @pl.when(ki < qi)
    def _():
        step(False)

    @pl.when(ki == qi)
    def _():
        step(True)
        lam = lam_ref[0]
        o = acc_scr[0] / l_scr[0] - lam * (acc_scr[1] / l_scr[1])
        ms = jnp.mean(o * o, axis=-1, keepdims=True)
        o = o * lax.rsqrt(ms + RMS_EPS) * gain_ref[...] * out_scale
        o_ref[...] = o.astype(o_ref.dtype)


def attn_prompt(p, prev, slopes, lam, gain, *, nbatch, seqlen, tq, col_q, col_k, col_v, out_col, dk, dv,
                out_scale):
    nq = seqlen // tq
    pairs = [(i, j) for i in range(nq) for j in range(i + 1)]
    qi_tab = jnp.asarray([i for i, _ in pairs], jnp.int32)
    ki_tab = jnp.asarray([j for _, j in pairs], jnp.int32)
    kern = functools.partial(_attn_prompt_kernel, tq=tq, tk=tq, dk=dk, dv=dv, out_scale=out_scale)
    smem = pl.BlockSpec(memory_space=pltpu.SMEM)
    grid_spec = pltpu.PrefetchScalarGridSpec(
        num_scalar_prefetch=2,
        grid=(nbatch, DIFF_HEADS, len(pairs)),
        in_specs=[smem, smem,
                  pl.BlockSpec((tq, 2 * dk), lambda b, h, s, qt, kt: (b * nq + qt[s], col_q // (2 * dk) + h)),
                  pl.BlockSpec((tq, 2 * dk), lambda b, h, s, qt, kt: (b * nq + kt[s], col_k // (2 * dk) + h)),
                  pl.BlockSpec((tq, dv), lambda b, h, s, qt, kt: (b * nq + kt[s], col_v // dv + h)),
                  pl.BlockSpec((1, dv), lambda b, h, s, qt, kt: (0, 0)),
                  pl.BlockSpec(memory_space=pl.ANY)],
        out_specs=pl.BlockSpec((tq, dv), lambda b, h, s, qt, kt: (b * nq + qt[s], out_col // dv + h)),
        scratch_shapes=[pltpu.VMEM((2, tq, 1), F32), pltpu.VMEM((2, tq, 1), F32),
                        pltpu.VMEM((2, tq, dv), F32)],
    )
    return pl.pallas_call(
        kern,
        grid_spec=grid_spec,
        out_shape=jax.ShapeDtypeStruct(prev.shape, prev.dtype),
        input_output_aliases={8: 0},
        compiler_params=_cparams(("parallel", "parallel", "arbitrary")),
        name="attn_prompt",
    )(qi_tab, ki_tab, slopes, lam, p, p, p, gain, prev)


def _attn_sample_kernel(pt_ref, lam_ref, q_ref, kn_ref, vn_ref, gain_ref, c0n_ref, c0s_ref, scol_ref, *rest,
                        pg, tnew, past_len, dk, dv, out_scale):
    k_refs = rest[:pg]
    v_refs = rest[pg:2 * pg]
    o_ref = rest[2 * pg]
    qt_scr, m_scr, l_scr, acc_scr = rest[2 * pg + 1:]
    j = pl.program_id(1)
    nj = pl.num_programs(1)
    nh = DIFF_HEADS
    half = nh * tnew
    page_lanes = PAGE_SIZE * nh

    @pl.when(j == 0)
    def _():
        s_parts = []
        for c in range(2):
            cols = [slice((2 * h + c) * dk, (2 * h + c + 1) * dk) for h in range(nh)]
            qt = jnp.concatenate([q_ref[:, cs] for cs in cols], axis=0) * (dk ** -0.5 * LOG2E)
            qt_scr[c] = qt.astype(BF16)
            kn = jnp.concatenate([kn_ref[:, cs] for cs in cols], axis=0)
            s_parts.append(lax.dot_general(qt, kn, NT_DIMS, preferred_element_type=F32))
        s = jnp.concatenate(s_parts, axis=0) - c0n_ref[...]
        m = jnp.max(s, axis=-1, keepdims=True)
        p = jnp.exp2(s - m)
        vn = jnp.concatenate([vn_ref[:, h * dv:(h + 1) * dv] for h in range(nh)], axis=0)
        m_scr[...] = m
        l_scr[...] = jnp.sum(p, axis=-1, keepdims=True)
        acc_scr[...] = jnp.dot(p, vn, preferred_element_type=F32)

    s_parts = []
    for c in range(2):
        qt = qt_scr[c]
        s_parts.append(jnp.concatenate(
            [lax.dot_general(qt, k_refs[i][pl.ds(c, page_lanes, stride=2), :].astype(BF16), NT_DIMS,
                             preferred_element_type=F32) for i in range(pg)], axis=-1))
    s = jnp.concatenate(s_parts, axis=0) - c0s_ref[...]
    addcol = scol_ref[...] * (j * (pg * PAGE_SIZE) - past_len).astype(F32)
    m_prev = m_scr[...]
    m_new = jnp.maximum(m_prev, jnp.max(s, axis=-1, keepdims=True) + addcol)
    alpha = jnp.exp2(m_prev - m_new)
    p = jnp.exp2(s + (addcol - m_new))
    l_scr[...] = alpha * l_scr[...] + jnp.sum(p, axis=-1, keepdims=True)
    pb = p.astype(BF16)
    acc = alpha * acc_scr[...]
    for i in range(pg):
        vi = v_refs[i][...].reshape(page_lanes, dv).astype(BF16)
        acc = acc + jnp.dot(pb[:, i * page_lanes:(i + 1) * page_lanes], vi, preferred_element_type=F32)
    acc_scr[...] = acc
    m_scr[...] = m_new

    @pl.when(j == nj - 1)
    def _():
        lam = lam_ref[0]
        on = acc / l_scr[...]
        for h in range(nh):
            o = on[h * tnew:(h + 1) * tnew, :] - lam * on[half + h * tnew:half + (h + 1) * tnew, :]
            ms = jnp.mean(o * o, axis=-1, keepdims=True)
            o_ref[:, h * dv:(h + 1) * dv] = o * lax.rsqrt(ms + RMS_EPS) * gain_ref[...] * out_scale


def _sample_bias_tiles(slopes, tnew, pg):
    nh = DIFF_HEADS
    r = np.arange(2 * nh * tnew)
    rh, rq = (r // tnew) % nh, r % tnew
    sl = np.asarray(slopes, np.float64)[rh] * LOG2E
    ln = np.arange(nh * tnew)
    lh, lt = ln // tnew, ln % tnew
    ok = (lh[None, :] == rh[:, None]) & (lt[None, :] <= rq[:, None])
    c0n = np.where(ok, sl[:, None] * (rq[:, None] - lt[None, :]), np.inf)
    ls = np.arange(pg * PAGE_SIZE * nh)
    lh, lt = ls % nh, ls // nh
    ok = lh[None, :] == rh[:, None]
    c0s = np.where(ok, sl[:, None] * (rq[:, None] - lt[None, :]), np.inf)
    return (jnp.asarray(c0n, F32), jnp.asarray(c0s, F32), jnp.asarray(sl[:, None], F32))


def attn_sample(p, cache_k, cache_v, page_table, slopes, lam, gain, *, nbatch, tnew, row0, pg,
                colblk_q, colblk_k, colblk_v, dk, dv, out_scale):
    n_pages = page_table.shape[1]
    nj = n_pages // pg
    nh = DIFF_HEADS
    width = nh * dv
    rb0 = row0 // tnew
    rows = 2 * nh * tnew
    c0n, c0s, scol = _sample_bias_tiles(slopes, tnew, pg)
    kern = functools.partial(_attn_sample_kernel, pg=pg, tnew=tnew, past_len=n_pages * PAGE_SIZE,
                             dk=dk, dv=dv, out_scale=out_scale)
    smem = pl.BlockSpec(memory_space=pltpu.SMEM)

    def kpage_spec(i):
        return pl.BlockSpec((None, PAGE_SIZE * nh * 2, dk), lambda b, j, pt: (pt[b, j * pg + i], 0, 0))

    def vpage_spec(i):
        return pl.BlockSpec((None, PAGE_SIZE, nh, dv), lambda b, j, pt: (pt[b, j * pg + i], 0, 0, 0))

    in_specs = [smem,
                pl.BlockSpec((tnew, width), lambda b, j, pt: (rb0 + b, colblk_q)),
                pl.BlockSpec((tnew, width), lambda b, j, pt: (rb0 + b, colblk_k)),
                pl.BlockSpec((tnew, width), lambda b, j, pt: (rb0 + b, colblk_v)),
                pl.BlockSpec((1, dv), lambda b, j, pt: (0, 0)),
                pl.BlockSpec(c0n.shape, lambda b, j, pt: (0, 0)),
                pl.BlockSpec(c0s.shape, lambda b, j, pt: (0, 0)),
                pl.BlockSpec((rows, 1), lambda b, j, pt: (0, 0))]
    in_specs += [kpage_spec(i) for i in range(pg)] + [vpage_spec(i) for i in range(pg)]
    grid_spec = pltpu.PrefetchScalarGridSpec(
        num_scalar_prefetch=1,
        grid=(nbatch, nj),
        in_specs=in_specs,
        out_specs=pl.BlockSpec((tnew, width), lambda b, j, pt: (b, 0)),
        scratch_shapes=[pltpu.VMEM((2, nh * tnew, dk), BF16),
                        pltpu.VMEM((rows, 1), F32),
                        pltpu.VMEM((rows, 1), F32),
                        pltpu.VMEM((rows, dv), F32)],
    )
    return pl.pallas_call(
        kern,
        grid_spec=grid_spec,
        out_shape=jax.ShapeDtypeStruct((nbatch * tnew, width), F32),
        compiler_params=_cparams(("parallel", "arbitrary")),
        name="attn_sample",
    )(page_table, lam, p, p, p, gain, c0n, c0s, scol, *([cache_k] * pg), *([cache_v] * pg))


N_TOP = PEER_TOPK + 1


def _top_values(x, n):
    cur = jnp.max(x, axis=0, keepdims=True)
    vals = [cur]
    for _ in range(n - 1):
        cur = jnp.max(jnp.where(x < cur, x, NEG_INF), axis=0, keepdims=True)
        vals.append(cur)
    return vals


_CAND_PAIRS = [(a, b) for a in range(N_TOP) for b in range(N_TOP) if (a + 1) * (b + 1) <= N_TOP]
_CAND_ROWS = -(-len(_CAND_PAIRS) // SUBLANES) * SUBLANES


def _route_kernel(qt_ref, k1_ref, k2_ref, a1_ref, s2_ref, th_ref, cand_scr):
    half = PEER_DQ // 2
    tn = qt_ref.shape[1]
    cand_scr[...] = jnp.full_like(cand_scr, NEG_INF)
    for h in range(PEER_HEADS):
        q1 = qt_ref[h * PEER_DQ:h * PEER_DQ + half, :]
        q2 = qt_ref[h * PEER_DQ + half:(h + 1) * PEER_DQ, :]
        s1 = jnp.dot(k1_ref[...], q1, preferred_element_type=F32, precision=lax.Precision.HIGHEST)
        s2 = jnp.dot(k2_ref[...], q2, preferred_element_type=F32, precision=lax.Precision.HIGHEST)
        t1 = _top_values(s1, N_TOP)
        t2 = _top_values(s2, N_TOP)
        for i, (a, b) in enumerate(_CAND_PAIRS):
            cand_scr[i:i + 1, :] = t1[a] + t2[b]
        c = _top_values(cand_scr[...], N_TOP)
        m = c[0]
        z = jnp.zeros_like(m)
        for i in range(PEER_TOPK):
            z = z + jnp.exp(c[i] - m)
        shift = m + jnp.log(z) + math.log(2.0)
        thr = (0.5 * (c[PEER_TOPK - 1] + c[PEER_TOPK]) - shift) * LOG2E
        a1 = (s1 - shift) * LOG2E
        s2 = s2 * LOG2E
        for cb in range(tn // LANES):
            a1_ref[h, cb] = a1[:, cb * LANES:(cb + 1) * LANES]
            s2_ref[h, cb] = s2[:, cb * LANES:(cb + 1) * LANES]
            th_ref[cb, h:h + 1, :] = thr[:, cb * LANES:(cb + 1) * LANES]


def peer_route(qt, keys1, keys2, tn):
    n = qt.shape[1]
    nk = PEER_NKEYS
    nb = tn // LANES
    tile4 = pl.BlockSpec((PEER_HEADS, nb, nk, LANES), lambda i: (0, i, 0, 0))
    shape4 = jax.ShapeDtypeStruct((PEER_HEADS, n // LANES, nk, LANES), F32)
    return pl.pallas_call(
        _route_kernel,
        grid=(n // tn,),
        in_specs=[pl.BlockSpec((PEER_HEADS * PEER_DQ, tn), lambda i: (0, i)),
                  pl.BlockSpec((nk, PEER_DQ // 2), lambda i: (0, 0)),
                  pl.BlockSpec((nk, PEER_DQ // 2), lambda i: (0, 0))],
        out_specs=[tile4, tile4, pl.BlockSpec((nb, PEER_HEADS, LANES), lambda i: (i, 0, 0))],
        out_shape=[shape4, shape4, jax.ShapeDtypeStruct((n // LANES, PEER_HEADS, LANES), F32)],
        scratch_shapes=[pltpu.VMEM((_CAND_ROWS, tn), F32)],
        compiler_params=_cparams(("parallel",)),
        name="peer_route",
    )(qt, keys1, keys2)


_GELU_C = math.sqrt(2.0 / math.pi)


def _peer_act_kernel(x_ref, u_ref, s2_ref, a1_ref, th_ref, o_ref):
    te, tt = o_ref.shape
    nk = PEER_NKEYS
    pre = lax.dot_general(u_ref[...], x_ref[...], NT_DIMS, preferred_element_type=F32)
    for r in range(te // nk):
        for cb in range(tt // LANES):
            w = jnp.zeros((nk, LANES), F32)
            for h in range(PEER_HEADS):
                a1 = jnp.tile(jnp.broadcast_to(a1_ref[h, cb, r:r + 1, :], (SUBLANES, LANES)), (nk // SUBLANES, 1))
                th = jnp.tile(jnp.broadcast_to(th_ref[cb, h:h + 1, :], (SUBLANES, LANES)), (nk // SUBLANES, 1))
                d = s2_ref[h, cb] + a1
                w = w + jnp.where(d > th, jnp.exp2(d), 0.0)
            x = pre[r * nk:(r + 1) * nk, cb * LANES:(cb + 1) * LANES]
            g = x * (1.0 + jnp.tanh(_GELU_C * (x + 0.044715 * (x * x * x))))
            o_ref[r * nk:(r + 1) * nk, cb * LANES:(cb + 1) * LANES] = (g * w).astype(o_ref.dtype)


def peer_act(xn, u, a1t, s2t, tht, *, tt, te):
    n, d = xn.shape
    ne = u.shape[0]
    rows = te // PEER_NKEYS
    nb = tt // LANES
    return pl.pallas_call(
        _peer_act_kernel,
        grid=(n // tt, ne // te),
        in_specs=[pl.BlockSpec((tt, d), lambda t, e: (t, 0)),
                  pl.BlockSpec((te, d), lambda t, e: (e, 0)),
                  pl.BlockSpec((PEER_HEADS, nb, PEER_NKEYS, LANES), lambda t, e: (0, t, 0, 0)),
                  pl.BlockSpec((PEER_HEADS, nb, rows, LANES), lambda t, e: (0, t, e, 0)),
                  pl.BlockSpec((nb, PEER_HEADS, LANES), lambda t, e: (t, 0, 0))],
        out_specs=pl.BlockSpec((te, tt), lambda t, e: (e, t)),
        out_shape=jax.ShapeDtypeStruct((ne, n), BF16),
        compiler_params=_cparams(("parallel", "arbitrary")),
        name="peer_act",
    )(xn, u, s2t, a1t, tht)


def _final_kernel(h_ref, pt_ref, g_ref, o_ref):
    x = h_ref[...] + pt_ref[...].T
    ms = jnp.mean(x * x, axis=-1, keepdims=True)
    o_ref[...] = x * lax.rsqrt(ms + RMS_EPS) * g_ref[...]


def final_norm(h, peer_t, g, *, col0, tm):
    nrows, d = h.shape
    cb0 = col0 // tm
    return pl.pallas_call(
        _final_kernel,
        grid=(nrows // tm,),
        in_specs=[pl.BlockSpec((tm, d), lambda i: (i, 0)),
                  pl.BlockSpec((d, tm), lambda i: (0, cb0 + i)),
                  pl.BlockSpec((1, d), lambda i: (0, 0))],
        out_specs=pl.BlockSpec((tm, d), lambda i: (i, 0)),
        out_shape=jax.ShapeDtypeStruct((nrows, d), F32),
        compiler_params=_cparams(("parallel",)),
        name="final_norm",
    )(h, peer_t, g.reshape(1, d))


def _tiles(n, n_p, n_s):
    def pick(cands, m=n):
        for c in cands:
            if m % c == 0:
                return c
        raise ValueError(f"no tile for {m}")
    both = math.gcd(n_p, n_s)
    return dict(
        rms_tm=pick((256, 128, 64, 32, 16), both),
        mm_tm=pick((768, 512, 256, 128, 64, 32, 16)),
        mm_tm_p=pick((1024, 512, 256, 128, 64, 32, 16), n_p),
        mm_tm_s=pick((256, 128, 64, 32, 16), n_s),
        peer_tt=pick((768, 512, 256, 128)),
        route_tn=pick((256, 128)),
        final_tm=pick((256, 128), both),
    )


def _alibi_slopes(nheads):
    return [2.0 ** (-8.0 * (i + 1) / nheads) for i in range(nheads)]


def kernel(x_prompt, x_sample, cache_k, cache_v, state_gla, page_table, norm1_gain, w_in, w_a2, b_a, gla_gain,
           lambda_q1, lambda_k1, lambda_q2, lambda_k2, diff_gain, w_o, norm2_gain, peer_wq, peer_keys1,
           peer_keys2, peer_u, peer_v, final_gain):
    depth = w_in.shape[0]
    assert depth == 1
    B, T, D = x_prompt.shape
    DB, TS, _ = x_sample.shape
    n_p, n_s = B * T, DB * TS
    n = n_p + n_s
    gla_dv = D // (2 * GLA_HEADS)
    gla_dk = gla_dv // 2
    diff_dv = D // (2 * DIFF_HEADS)
    diff_dk = diff_dv // 2
    gqk_w = GLA_HEADS * gla_dk
    gv_w = GLA_HEADS * gla_dv
    dqk_w = DIFF_HEADS * 2 * diff_dk
    dv_w = DIFF_HEADS * diff_dv
    g_w = 2 * gqk_w + 2 * gv_w
    tl = _tiles(n, n_p, n_s)
    l = 0
    lam_init = 0.8 - 0.6 * math.exp(-0.3 * l)
    xp = x_prompt.reshape(n_p, D)
    xs = x_sample.reshape(n_s, D)

    wb = w_in[l].astype(BF16)
    w_g = wb[:, :g_w]
    w_d = wb[:, g_w + GLA_LOWRANK:]
    w_ga = jnp.pad(wb[:, g_w:g_w + GLA_LOWRANK], ((0, 0), (0, LANES - GLA_LOWRANK)))
    wa2p = jnp.pad(w_a2[l], ((0, LANES - GLA_LOWRANK), (0, 0)))
    w_ob = w_o[l].astype(BF16)
    f32 = F32
    lam = (jnp.exp(jnp.sum(lambda_q1[l].astype(f32) * lambda_k1[l].astype(f32)))
           - jnp.exp(jnp.sum(lambda_q2[l].astype(f32) * lambda_k2[l].astype(f32))) + lam_init).reshape(1)
    slopes = _alibi_slopes(DIFF_HEADS)

    xn = rms_norm([xp, xs], norm1_gain[l], tl["rms_tm"])
    proj_g = matmul(xn, w_g, tl["mm_tm"], 1024, name="proj_gla")
    proj_d = matmul(xn, w_d, tl["mm_tm"], 1024, name="proj_diff")
    ga = matmul(xn, w_ga, tl["mm_tm"], LANES, name="proj_gate")

    gla_args = (proj_g, ga, wa2p, b_a[l].reshape(1, -1), gla_gain[l].reshape(1, -1))
    mix_p, s_p = gla(*gla_args, None, nbatch=B, seqlen=T, chunk=64, valid=64, row0=0, out_cols=D,
                     out_dtype=BF16, dk=gla_dk, dv=gla_dv, name="gla_prompt")
    og_s, s_s = gla(*gla_args, state_gla[l], nbatch=DB, seqlen=TS, chunk=GLA_SUB, valid=TS, row0=n_p,
                    out_cols=gv_w, out_dtype=F32, dk=gla_dk, dv=gla_dv, name="gla_sample")

    out_scale = 1.0 - lam_init
    dgain = diff_gain[l].reshape(1, -1)
    mix_p = attn_prompt(proj_d, mix_p, jnp.asarray(slopes, F32), lam, dgain, nbatch=B, seqlen=T,
                        tq=min(512, T), col_q=0, col_k=dqk_w, col_v=2 * dqk_w, out_col=gv_w,
                        dk=diff_dk, dv=diff_dv, out_scale=out_scale)
    n_pool = cache_k.shape[1]
    ck = cache_k[l].reshape(n_pool, PAGE_SIZE * DIFF_HEADS * 2, diff_dk)
    od_s = attn_sample(proj_d, ck, cache_v[l], page_table, slopes, lam, dgain, nbatch=DB, tnew=TS, row0=n_p,
                       pg=4, colblk_q=0, colblk_k=1, colblk_v=2, dk=diff_dk, dv=diff_dv, out_scale=out_scale)
    mix_s = jnp.concatenate([og_s, od_s], axis=1).astype(BF16)

    h_p = matmul(mix_p, w_ob, tl["mm_tm_p"], 1024, res=xp, name="out_proj")
    h_s = matmul(mix_s, w_ob, tl["mm_tm_s"], 1024, res=xs, name="out_proj_sample")

    xn2 = rms_norm([h_p, h_s], norm2_gain[l], tl["rms_tm"])
    qt = matmul_nt(peer_wq[l].T.astype(BF16), xn2, PEER_HEADS * PEER_DQ, tl["peer_tt"], name="peer_query")
    a1t, s2t, tht = peer_route(qt, peer_keys1[l], peer_keys2[l], tl["route_tn"])
    act_t = peer_act(xn2, peer_u[l].astype(BF16), a1t, s2t, tht, tt=tl["peer_tt"], te=1024)
    peer_t = matmul_kacc(peer_v[l].T.astype(BF16), act_t, D, tl["peer_tt"], 1024, name="peer_out")

    y_p = final_norm(h_p, peer_t, final_gain, col0=0, tm=tl["final_tm"])
    y_s = final_norm(h_s, peer_t, final_gain, col0=n_p, tm=tl["final_tm"])

    k_p = proj_d[:n_p, dqk_w:2 * dqk_w].reshape(1, B, T, DIFF_HEADS, 2, diff_dk)
    v_p = proj_d[:n_p, 2 * dqk_w:].reshape(1, B, T, DIFF_HEADS, diff_dv)
    k_s = proj_d[n_p:, dqk_w:2 * dqk_w].reshape(1, DB, TS, DIFF_HEADS, 2, diff_dk)
    v_s = proj_d[n_p:, 2 * dqk_w:].reshape(1, DB, TS, DIFF_HEADS, diff_dv)
    return (y_p.reshape(B, T, D), y_s.reshape(DB, TS, D), k_p, v_p, s_p[None], k_s, v_s, s_s[None])
```

```python
import functools
import math

import jax
import jax.numpy as jnp
import numpy as np
from jax import lax
from jax.experimental import pallas as pl
from jax.experimental.pallas import tpu as pltpu

F32 = jnp.float32
BF16 = jnp.bfloat16

GLA_HEADS = 4
GLA_LOWRANK = 16
GLA_GATE_TEMP = 16.0
DIFF_HEADS = 8
PEER_HEADS = 8
PEER_NKEYS = 128
PEER_DQ = 128
PEER_TOPK = 16
PAGE_SIZE = 128
RMS_EPS = 1e-6
LANES = 128
SUBLANES = 8
VMEM_LIMIT = 56 * 1024 * 1024
LOG2E = math.log2(math.e)

NT_DIMS = (((1,), (1,)), ((), ()))
TN_DIMS = (((0,), (0,)), ((), ()))
NEG_INF = float("-inf")


def _cparams(sem):
    return pltpu.CompilerParams(dimension_semantics=sem, vmem_limit_bytes=VMEM_LIMIT)


def _rms_kernel(x_ref, g_ref, *rest):
    o_ref = rest[-1]
    x = x_ref[...]
    ms = jnp.mean(x * x, axis=-1, keepdims=True)
    o_ref[...] = (x * lax.rsqrt(ms + RMS_EPS) * g_ref[...]).astype(o_ref.dtype)


def rms_norm(parts, g, tm, out_dtype=BF16):
    d = parts[0].shape[1]
    total = sum(p.shape[0] for p in parts)
    out = None
    row0 = 0
    for x in parts:
        n = x.shape[0]
        rb0 = row0 // tm
        in_specs = [pl.BlockSpec((tm, d), lambda i: (i, 0)),
                    pl.BlockSpec((1, d), lambda i: (0, 0))]
        args = [x, g.reshape(1, d)]
        aliases = {}
        if out is not None:
            in_specs.append(pl.BlockSpec(memory_space=pl.ANY))
            args.append(out)
            aliases = {2: 0}
        out = pl.pallas_call(
            _rms_kernel,
            grid=(n // tm,),
            in_specs=in_specs,
            out_specs=pl.BlockSpec((tm, d), lambda i, rb0=rb0: (rb0 + i, 0)),
            out_shape=jax.ShapeDtypeStruct((total, d), out_dtype),
            input_output_aliases=aliases,
            compiler_params=_cparams(("parallel",)),
            name="rms_norm",
        )(*args)
        row0 += n
    return out


def _mm_kernel(x_ref, w_ref, o_ref):
    o_ref[...] = jnp.dot(x_ref[...], w_ref[...], preferred_element_type=F32)


def _mm_res_kernel(x_ref, w_ref, r_ref, o_ref):
    o_ref[...] = r_ref[...] + jnp.dot(x_ref[...], w_ref[...], preferred_element_type=F32)


def matmul(x, w, tm, tn, res=None, name="matmul"):
    n, k = x.shape
    m = w.shape[1]
    in_specs = [pl.BlockSpec((tm, k), lambda i, j: (i, 0)),
                pl.BlockSpec((k, tn), lambda i, j: (0, j))]
    args = [x, w]
    kern = _mm_kernel
    if res is not None:
        in_specs.append(pl.BlockSpec((tm, tn), lambda i, j: (i, j)))
        args.append(res)
        kern = _mm_res_kernel
    return pl.pallas_call(
        kern,
        grid=(n // tm, m // tn),
        in_specs=in_specs,
        out_specs=pl.BlockSpec((tm, tn), lambda i, j: (i, j)),
        out_shape=jax.ShapeDtypeStruct((n, m), F32),
        compiler_params=_cparams(("parallel", "parallel")),
        name=name,
    )(*args)


def _mm_nt_kernel(a_ref, b_ref, o_ref):
    o_ref[...] = lax.dot_general(a_ref[...], b_ref[...], NT_DIMS, preferred_element_type=F32)


def matmul_nt(a, b, tm, tn, name="matmul_nt"):
    m, k = a.shape
    n = b.shape[0]
    return pl.pallas_call(
        _mm_nt_kernel,
        grid=(n // tn, m // tm),
        in_specs=[pl.BlockSpec((tm, k), lambda j, i: (i, 0)),
                  pl.BlockSpec((tn, k), lambda j, i: (j, 0))],
        out_specs=pl.BlockSpec((tm, tn), lambda j, i: (i, j)),
        out_shape=jax.ShapeDtypeStruct((m, n), F32),
        compiler_params=_cparams(("parallel", "parallel")),
        name=name,
    )(a, b)


def _mm_acc_kernel(a_ref, b_ref, o_ref):
    @pl.when(pl.program_id(2) == 0)
    def _():
        o_ref[...] = jnp.zeros_like(o_ref)

    tm = o_ref.shape[0]
    rc = min(tm, 1024)
    for r in range(tm // rc):
        o_ref[r * rc:(r + 1) * rc, :] += jnp.dot(a_ref[r * rc:(r + 1) * rc, :], b_ref[...],
                                                 preferred_element_type=F32)


def matmul_kacc(a, b, tm, tn, tk, name="matmul_kacc"):
    m, k = a.shape
    n = b.shape[1]
    return pl.pallas_call(
        _mm_acc_kernel,
        grid=(m // tm, n // tn, k // tk),
        in_specs=[pl.BlockSpec((tm, tk), lambda i, j, kk: (i, kk)),
                  pl.BlockSpec((tk, tn), lambda i, j, kk: (kk, j))],
        out_specs=pl.BlockSpec((tm, tn), lambda i, j, kk: (i, j)),
        out_shape=jax.ShapeDtypeStruct((m, n), F32),
        compiler_params=_cparams(("parallel", "parallel", "arbitrary")),
        name=name,
    )(a, b)


GLA_SUB = 16


def _log_sigmoid(x):
    return jnp.minimum(x, 0.0) - jnp.log1p(jnp.exp(-jnp.abs(x)))


def _gla_chunk(q, k, v, r, ga, wa, ba, gain, sts, *, chunk, valid, dk, dv):
    nh = GLA_HEADS
    hk = [slice(h * dk, (h + 1) * dk) for h in range(nh)]
    hv = [slice(h * dv, (h + 1) * dv) for h in range(nh)]
    q = q * (dk ** -0.5)
    pre = jnp.dot(ga, wa, preferred_element_type=F32, precision=lax.Precision.HIGHEST) + ba
    g = _log_sigmoid(pre) * (1.0 / GLA_GATE_TEMP)
    if valid < chunk:
        g = jnp.where(lax.broadcasted_iota(jnp.int32, (chunk, 1), 0) < valid, g, 0.0)
    ri = lax.broadcasted_iota(jnp.int32, (chunk, chunk), 0)
    ci = lax.broadcasted_iota(jnp.int32, (chunk, chunk), 1)
    tril = (ci <= ri).astype(F32)
    G = jnp.dot(tril, g, preferred_element_type=F32, precision=lax.Precision.HIGHEST)
    g_last = G[chunk - 1:chunk, :]

    qg = (q * jnp.exp(G)).astype(BF16)
    o = [lax.dot_general(qg[:, hk[h]], sts[h].astype(BF16), NT_DIMS, preferred_element_type=F32)
         for h in range(nh)]

    sub = min(GLA_SUB, chunk)
    lane = lax.broadcasted_iota(jnp.int32, (sub, chunk), 1)
    srow = lax.broadcasted_iota(jnp.int32, (sub, chunk), 0)
    a_rows = [[] for _ in range(nh)]
    for i in range(chunk // sub):
        lo = i * sub
        Gi = G[lo:lo + sub, :]
        qi = q[lo:lo + sub, :]
        if i > 0:
            ref_i = G[lo - 1:lo, :]
            qs = (qi * jnp.exp(Gi - ref_i)).astype(BF16)
            ks = (k * jnp.exp(jnp.minimum(ref_i - G, 0.0))).astype(BF16)
            a_i = [jnp.where(lane < lo, lax.dot_general(qs[:, hk[h]], ks[:, hk[h]], NT_DIMS,
                                                        preferred_element_type=F32), 0.0) for h in range(nh)]
        else:
            a_i = [jnp.zeros((sub, chunk), F32) for _ in range(nh)]
        for s in range(sub):
            w = qi * jnp.exp(jnp.minimum(Gi - G[lo + s:lo + s + 1, :], 0.0)) * k[lo + s:lo + s + 1, :]
            for h in range(nh):
                col = jnp.sum(w[:, hk[h]], axis=1, keepdims=True)
                a_i[h] = jnp.where(lane == lo + s, col, a_i[h])
        for h in range(nh):
            a_rows[h].append(jnp.where(lane <= lo + srow, a_i[h], 0.0))

    kd = (k * jnp.exp(g_last - G)).astype(BF16)
    decay = jnp.exp(g_last)
    vb = v.astype(BF16)
    rg = r * jax.nn.sigmoid(r)
    outs, sts_new = [], []
    for h in range(nh):
        a_mat = a_rows[h][0] if len(a_rows[h]) == 1 else jnp.concatenate(a_rows[h], axis=0)
        oh = o[h] + jnp.dot(a_mat.astype(BF16), vb[:, hv[h]], preferred_element_type=F32)
        upd = lax.dot_general(vb[:, hv[h]], kd[:, hk[h]], TN_DIMS, preferred_element_type=F32)
        sts_new.append(sts[h] * decay[:, hk[h]] + upd)
        oh = oh[:valid, :]
        ms = jnp.mean(oh * oh, axis=-1, keepdims=True)
        outs.append(oh * lax.rsqrt(ms + RMS_EPS) * gain[:, hv[h]] * rg[:, hv[h]])
    return outs, sts_new


def _gla_kernel(*refs, chunk, valid, has_s0, dk, dv):
    if has_s0:
        (q_ref, k_ref, v_ref, r_ref, ga_ref, wa_ref, ba_ref, gain_ref, s0_ref, o_ref, s_ref, st_scr) = refs
    else:
        (q_ref, k_ref, v_ref, r_ref, ga_ref, wa_ref, ba_ref, gain_ref, o_ref, s_ref, st_scr) = refs
        s0_ref = None
    c = pl.program_id(1)
    nc = pl.num_programs(1)

    @pl.when(c == 0)
    def _():
        for h in range(GLA_HEADS):
            if has_s0:
                st_scr[h] = s0_ref[h].T
            else:
                st_scr[h] = jnp.zeros((dv, dk), F32)

    def rows(x):
        if valid < chunk:
            x = jnp.concatenate([x, jnp.zeros((chunk - valid, x.shape[1]), x.dtype)], axis=0)
        return x

    outs, sts_new = _gla_chunk(rows(q_ref[...]), rows(k_ref[...]), rows(v_ref[...]), r_ref[...],
                               rows(ga_ref[...]), wa_ref[...], ba_ref[...], gain_ref[...],
                               [st_scr[h] for h in range(GLA_HEADS)], chunk=chunk, valid=valid, dk=dk, dv=dv)
    for h in range(GLA_HEADS):
        st_scr[h] = sts_new[h]
        o_ref[:, h * dv:(h + 1) * dv] = outs[h].astype(o_ref.dtype)

    @pl.when(c == nc - 1)
    def _():
        for h in range(GLA_HEADS):
            s_ref[h] = sts_new[h].T


def gla(p, ga, wa2p, b_a, gain, s0, *, nbatch, seqlen, chunk, valid, row0, out_cols, out_dtype, dk, dv, name):
    nc = seqlen // valid
    rb0 = row0 // valid
    h_n = GLA_HEADS
    qk_w, v_w = h_n * dk, h_n * dv
    assert v_w == 2 * qk_w
    has_s0 = s0 is not None

    def rowblk(b, c):
        return rb0 + b * nc + c

    in_specs = [
        pl.BlockSpec((valid, qk_w), lambda b, c: (rowblk(b, c), 0)),
        pl.BlockSpec((valid, qk_w), lambda b, c: (rowblk(b, c), 1)),
        pl.BlockSpec((valid, v_w), lambda b, c: (rowblk(b, c), 1)),
        pl.BlockSpec((valid, v_w), lambda b, c: (rowblk(b, c), 2)),
        pl.BlockSpec((valid, LANES), lambda b, c: (rowblk(b, c), 0)),
        pl.BlockSpec((LANES, qk_w), lambda b, c: (0, 0)),
        pl.BlockSpec((1, qk_w), lambda b, c: (0, 0)),
        pl.BlockSpec((1, v_w), lambda b, c: (0, 0)),
    ]
    args = [p, p, p, p, ga, wa2p, b_a, gain]
    if has_s0:
        in_specs.append(pl.BlockSpec((None, h_n, dk, dv), lambda b, c: (b, 0, 0, 0)))
        args.append(s0)
    kern = functools.partial(_gla_kernel, chunk=chunk, valid=valid, has_s0=has_s0, dk=dk, dv=dv)
    return pl.pallas_call(
        kern,
        grid=(nbatch, nc),
        in_specs=in_specs,
        out_specs=[pl.BlockSpec((valid, v_w), lambda b, c: (b * nc + c, 0)),
                   pl.BlockSpec((None, h_n, dk, dv), lambda b, c: (b, 0, 0, 0))],
        out_shape=[jax.ShapeDtypeStruct((nbatch * seqlen, out_cols), out_dtype),
                   jax.ShapeDtypeStruct((nbatch, h_n, dk, dv), F32)],
        scratch_shapes=[pltpu.VMEM((h_n, dv, dk), F32)],
        compiler_params=_cparams(("parallel", "arbitrary")),
        name=name,
    )(*args)


def _attn_prompt_kernel(qi_ref, ki_ref, slopes_ref, lam_ref, q_ref, k_ref, v_ref, gain_ref, prev_ref, o_ref,
                        m_scr, l_scr, acc_scr, *, tq, tk, dk, dv, out_scale):
    del prev_ref
    h = pl.program_id(1)
    step_id = pl.program_id(2)
    qi = qi_ref[step_id]
    ki = ki_ref[step_id]

    @pl.when(ki == 0)
    def _():
        m_scr[...] = jnp.full_like(m_scr, NEG_INF)
        l_scr[...] = jnp.zeros_like(l_scr)
        acc_scr[...] = jnp.zeros_like(acc_scr)

    def step(masked):
        krel = (ki * tk - qi * tq + lax.broadcasted_iota(jnp.int32, (1, tk), 1)).astype(F32)
        kbias = (slopes_ref[h] * LOG2E) * krel
        vb = v_ref[...].astype(BF16)
        if masked:
            keep = (lax.broadcasted_iota(jnp.int32, (tq, tk), 0)
                    >= lax.broadcasted_iota(jnp.int32, (tq, tk), 1))
        for c in range(2):
            qc = (q_ref[:, c * dk:(c + 1) * dk] * (dk ** -0.5 * LOG2E)).astype(BF16)
            kc = k_ref[:, c * dk:(c + 1) * dk].astype(BF16)
            s = lax.dot_general(qc, kc, NT_DIMS, preferred_element_type=F32) + kbias
            if masked:
                s = jnp.where(keep, s, NEG_INF)
            m_prev = m_scr[c]
            m_new = jnp.maximum(m_prev, jnp.max(s, axis=-1, keepdims=True))
            alpha = jnp.exp2(m_prev - m_new)
            p = jnp.exp2(s - jnp.tile(m_new, (1, tk // LANES)))
            l_scr[c] = alpha * l_scr[c] + jnp.sum(p, axis=-1, keepdims=True)
            acc_scr[c] = (jnp.tile(alpha, (1, dv // LANES)) * acc_scr[c]
                          + jnp.dot(p.astype(BF16), vb, preferred_element_type=F32))
            m_scr[c] = m_new

    @pl.when(ki < qi)
    def _():
        step(False)

    @pl.when(ki == qi)
    def _():
        step(True)
        lam = lam_ref[0]
        reps = (1, dv // LANES)
        o = acc_scr[0] / jnp.tile(l_scr[0], reps) - lam * (acc_scr[1] / jnp.tile(l_scr[1], reps))
        ms = jnp.mean(o * o, axis=-1, keepdims=True)
        o = o * lax.rsqrt(ms + RMS_EPS) * gain_ref[...] * out_scale
        o_ref[...] = o.astype(o_ref.dtype)


def attn_prompt(p, prev, slopes, lam, gain, *, nbatch, seqlen, tq, col_q, col_k, col_v, out_col, dk, dv,
                out_scale):
    nq = seqlen // tq
    pairs = [(i, j) for i in range(nq) for j in range(i + 1)]
    qi_tab = jnp.asarray([i for i, _ in pairs], jnp.int32)
    ki_tab = jnp.asarray([j for _, j in pairs], jnp.int32)
    kern = functools.partial(_attn_prompt_kernel, tq=tq, tk=tq, dk=dk, dv=dv, out_scale=out_scale)
    smem = pl.BlockSpec(memory_space=pltpu.SMEM)
    grid_spec = pltpu.PrefetchScalarGridSpec(
        num_scalar_prefetch=2,
        grid=(nbatch, DIFF_HEADS, len(pairs)),
        in_specs=[smem, smem,
                  pl.BlockSpec((tq, 2 * dk), lambda b, h, s, qt, kt: (b * nq + qt[s], col_q // (2 * dk) + h)),
                  pl.BlockSpec((tq, 2 * dk), lambda b, h, s, qt, kt: (b * nq + kt[s], col_k // (2 * dk) + h)),
                  pl.BlockSpec((tq, dv), lambda b, h, s, qt, kt: (b * nq + kt[s], col_v // dv + h)),
                  pl.BlockSpec((1, dv), lambda b, h, s, qt, kt: (0, 0)),
                  pl.BlockSpec(memory_space=pl.ANY)],
        out_specs=pl.BlockSpec((tq, dv), lambda b, h, s, qt, kt: (b * nq + qt[s], out_col // dv + h)),
        scratch_shapes=[pltpu.VMEM((2, tq, LANES), F32), pltpu.VMEM((2, tq, LANES), F32),
                        pltpu.VMEM((2, tq, dv), F32)],
    )
    return pl.pallas_call(
        kern,
        grid_spec=grid_spec,
        out_shape=jax.ShapeDtypeStruct(prev.shape, prev.dtype),
        input_output_aliases={8: 0},
        compiler_params=_cparams(("parallel", "parallel", "arbitrary")),
        name="attn_prompt",
    )(qi_tab, ki_tab, slopes, lam, p, p, p, gain, prev)


def _attn_sample_kernel(pt_ref, lam_ref, q_ref, kn_ref, vn_ref, gain_ref, c0n_ref, c0s_ref, scol_ref, *rest,
                        pg, tnew, past_len, dk, dv, out_scale):
    k_refs = rest[:pg]
    v_refs = rest[pg:2 * pg]
    o_ref = rest[2 * pg]
    qt_scr, m_scr, l_scr, acc_scr = rest[2 * pg + 1:]
    j = pl.program_id(1)
    nj = pl.num_programs(1)
    nh = DIFF_HEADS
    half = nh * tnew
    page_lanes = PAGE_SIZE * nh

    @pl.when(j == 0)
    def _():
        s_parts = []
        for c in range(2):
            cols = [slice((2 * h + c) * dk, (2 * h + c + 1) * dk) for h in range(nh)]
            qt = jnp.concatenate([q_ref[:, cs] for cs in cols], axis=0) * (dk ** -0.5 * LOG2E)
            qt_scr[c] = qt.astype(BF16)
            kn = jnp.concatenate([kn_ref[:, cs] for cs in cols], axis=0)
            s_parts.append(lax.dot_general(qt, kn, NT_DIMS, preferred_element_type=F32))
        s = jnp.concatenate(s_parts, axis=0) - c0n_ref[...]
        m = jnp.max(s, axis=-1, keepdims=True)
        p = jnp.exp2(s - m)
        vn = jnp.concatenate([vn_ref[:, h * dv:(h + 1) * dv] for h in range(nh)], axis=0)
        m_scr[...] = m
        l_scr[...] = jnp.sum(p, axis=-1, keepdims=True)
        acc_scr[...] = jnp.dot(p, vn, preferred_element_type=F32)

    s_parts = []
    for c in range(2):
        qt = qt_scr[c]
        s_parts.append(jnp.concatenate(
            [lax.dot_general(qt, k_refs[i][pl.ds(c, page_lanes, stride=2), :].astype(BF16), NT_DIMS,
                             preferred_element_type=F32) for i in range(pg)], axis=-1))
    s = jnp.concatenate(s_parts, axis=0) - c0s_ref[...]
    addcol = scol_ref[...] * (j * (pg * PAGE_SIZE) - past_len).astype(F32)
    m_prev = m_scr[...]
    m_new = jnp.maximum(m_prev, jnp.max(s, axis=-1, keepdims=True) + addcol)
    alpha = jnp.exp2(m_prev - m_new)
    p = jnp.exp2(s + (addcol - m_new))
    l_scr[...] = alpha * l_scr[...] + jnp.sum(p, axis=-1, keepdims=True)
    pb = p.astype(BF16)
    acc = alpha * acc_scr[...]
    for i in range(pg):
        vi = v_refs[i][...].reshape(page_lanes, dv).astype(BF16)
        acc = acc + jnp.dot(pb[:, i * page_lanes:(i + 1) * page_lanes], vi, preferred_element_type=F32)
    acc_scr[...] = acc
    m_scr[...] = m_new

    @pl.when(j == nj - 1)
    def _():
        lam = lam_ref[0]
        on = acc / l_scr[...]
        for h in range(nh):
            o = on[h * tnew:(h + 1) * tnew, :] - lam * on[half + h * tnew:half + (h + 1) * tnew, :]
            ms = jnp.mean(o * o, axis=-1, keepdims=True)
            o_ref[:, h * dv:(h + 1) * dv] = o * lax.rsqrt(ms + RMS_EPS) * gain_ref[...] * out_scale


def _sample_bias_tiles(slopes, tnew, pg):
    nh = DIFF_HEADS
    r = np.arange(2 * nh * tnew)
    rh, rq = (r // tnew) % nh, r % tnew
    sl = np.asarray(slopes, np.float64)[rh] * LOG2E
    ln = np.arange(nh * tnew)
    lh, lt = ln // tnew, ln % tnew
    ok = (lh[None, :] == rh[:, None]) & (lt[None, :] <= rq[:, None])
    c0n = np.where(ok, sl[:, None] * (rq[:, None] - lt[None, :]), np.inf)
    ls = np.arange(pg * PAGE_SIZE * nh)
    lh, lt = ls % nh, ls // nh
    ok = lh[None, :] == rh[:, None]
    c0s = np.where(ok, sl[:, None] * (rq[:, None] - lt[None, :]), np.inf)
    return (jnp.asarray(c0n, F32), jnp.asarray(c0s, F32), jnp.asarray(sl[:, None], F32))


def attn_sample(p, cache_k, cache_v, page_table, slopes, lam, gain, *, nbatch, tnew, row0, pg,
                colblk_q, colblk_k, colblk_v, dk, dv, out_scale):
    n_pages = page_table.shape[1]
    nj = n_pages // pg
    nh = DIFF_HEADS
    width = nh * dv
    rb0 = row0 // tnew
    rows = 2 * nh * tnew
    c0n, c0s, scol = _sample_bias_tiles(slopes, tnew, pg)
    kern = functools.partial(_attn_sample_kernel, pg=pg, tnew=tnew, past_len=n_pages * PAGE_SIZE,
                             dk=dk, dv=dv, out_scale=out_scale)
    smem = pl.BlockSpec(memory_space=pltpu.SMEM)

    def kpage_spec(i):
        return pl.BlockSpec((None, PAGE_SIZE * nh * 2, dk), lambda b, j, pt: (pt[b, j * pg + i], 0, 0))

    def vpage_spec(i):
        return pl.BlockSpec((None, PAGE_SIZE, nh, dv), lambda b, j, pt: (pt[b, j * pg + i], 0, 0, 0))

    in_specs = [smem,
                pl.BlockSpec((tnew, width), lambda b, j, pt: (rb0 + b, colblk_q)),
                pl.BlockSpec((tnew, width), lambda b, j, pt: (rb0 + b, colblk_k)),
                pl.BlockSpec((tnew, width), lambda b, j, pt: (rb0 + b, colblk_v)),
                pl.BlockSpec((1, dv), lambda b, j, pt: (0, 0)),
                pl.BlockSpec(c0n.shape, lambda b, j, pt: (0, 0)),
                pl.BlockSpec(c0s.shape, lambda b, j, pt: (0, 0)),
                pl.BlockSpec((rows, 1), lambda b, j, pt: (0, 0))]
    in_specs += [kpage_spec(i) for i in range(pg)] + [vpage_spec(i) for i in range(pg)]
    grid_spec = pltpu.PrefetchScalarGridSpec(
        num_scalar_prefetch=1,
        grid=(nbatch, nj),
        in_specs=in_specs,
        out_specs=pl.BlockSpec((tnew, width), lambda b, j, pt: (b, 0)),
        scratch_shapes=[pltpu.VMEM((2, nh * tnew, dk), BF16),
                        pltpu.VMEM((rows, 1), F32),
                        pltpu.VMEM((rows, 1), F32),
                        pltpu.VMEM((rows, dv), F32)],
    )
    return pl.pallas_call(
        kern,
        grid_spec=grid_spec,
        out_shape=jax.ShapeDtypeStruct((nbatch * tnew, width), F32),
        compiler_params=_cparams(("parallel", "arbitrary")),
        name="attn_sample",
    )(page_table, lam, p, p, p, gain, c0n, c0s, scol, *([cache_k] * pg), *([cache_v] * pg))


N_TOP = PEER_TOPK + 1


def _top_values(x, n):
    cur = jnp.max(x, axis=0, keepdims=True)
    vals = [cur]
    for _ in range(n - 1):
        cur = jnp.max(jnp.where(x < cur, x, NEG_INF), axis=0, keepdims=True)
        vals.append(cur)
    return vals


_CAND_PAIRS = [(a, b) for a in range(N_TOP) for b in range(N_TOP) if (a + 1) * (b + 1) <= N_TOP]
_CAND_ROWS = -(-len(_CAND_PAIRS) // SUBLANES) * SUBLANES


def _route_kernel(qt_ref, k1_ref, k2_ref, e1_ref, e2_ref, th_ref, cand_scr):
    half = PEER_DQ // 2
    tn = qt_ref.shape[1]
    cand_scr[...] = jnp.full_like(cand_scr, NEG_INF)
    for h in range(PEER_HEADS):
        q1 = qt_ref[h * PEER_DQ:h * PEER_DQ + half, :]
        q2 = qt_ref[h * PEER_DQ + half:(h + 1) * PEER_DQ, :]
        s1 = jnp.dot(k1_ref[...], q1, preferred_element_type=F32, precision=lax.Precision.HIGHEST)
        s2 = jnp.dot(k2_ref[...], q2, preferred_element_type=F32, precision=lax.Precision.HIGHEST)
        t1 = _top_values(s1, N_TOP)
        t2 = _top_values(s2, N_TOP)
        for i, (a, b) in enumerate(_CAND_PAIRS):
            cand_scr[i:i + 1, :] = t1[a] + t2[b]
        c = _top_values(cand_scr[...], N_TOP)
        m = c[0]
        z = jnp.zeros_like(m)
        for i in range(PEER_TOPK):
            z = z + jnp.exp(c[i] - m)
        scale = 0.5 / z
        e1 = jnp.exp(s1 - t1[0])
        e2 = jnp.exp(s2 - t2[0]) * scale
        thr = jnp.exp(0.5 * (c[PEER_TOPK - 1] + c[PEER_TOPK]) - m) * scale
        for cb in range(tn // LANES):
            e1_ref[h, cb] = e1[:, cb * LANES:(cb + 1) * LANES]
            e2_ref[h, cb] = e2[:, cb * LANES:(cb + 1) * LANES]
            th_ref[cb, h:h + 1, :] = thr[:, cb * LANES:(cb + 1) * LANES]


def peer_route(qt, keys1, keys2, tn):
    n = qt.shape[1]
    nk = PEER_NKEYS
    nb = tn // LANES
    tile4 = pl.BlockSpec((PEER_HEADS, nb, nk, LANES), lambda i: (0, i, 0, 0))
    shape4 = jax.ShapeDtypeStruct((PEER_HEADS, n // LANES, nk, LANES), F32)
    return pl.pallas_call(
        _route_kernel,
        grid=(n // tn,),
        in_specs=[pl.BlockSpec((PEER_HEADS * PEER_DQ, tn), lambda i: (0, i)),
                  pl.BlockSpec((nk, PEER_DQ // 2), lambda i: (0, 0)),
                  pl.BlockSpec((nk, PEER_DQ // 2), lambda i: (0, 0))],
        out_specs=[tile4, tile4, pl.BlockSpec((nb, PEER_HEADS, LANES), lambda i: (i, 0, 0))],
        out_shape=[shape4, shape4, jax.ShapeDtypeStruct((n // LANES, PEER_HEADS, LANES), F32)],
        scratch_shapes=[pltpu.VMEM((_CAND_ROWS, tn), F32)],
        compiler_params=_cparams(("parallel",)),
        name="peer_route",
    )(qt, keys1, keys2)


_GELU_C = math.sqrt(2.0 / math.pi)


def _peer_act_kernel(x_ref, u_ref, e2_ref, e1_ref, th_ref, o_ref):
    te, tt = o_ref.shape
    nk = PEER_NKEYS
    pre = lax.dot_general(u_ref[...], x_ref[...], NT_DIMS, preferred_element_type=F32)
    for r in range(te // nk):
        for cb in range(tt // LANES):
            w = jnp.zeros((nk, LANES), F32)
            for h in range(PEER_HEADS):
                e1 = jnp.tile(jnp.broadcast_to(e1_ref[h, cb, r:r + 1, :], (SUBLANES, LANES)), (nk // SUBLANES, 1))
                th = jnp.tile(jnp.broadcast_to(th_ref[cb, h:h + 1, :], (SUBLANES, LANES)), (nk // SUBLANES, 1))
                p = e2_ref[h, cb] * e1
                w = w + jnp.where(p > th, p, 0.0)
            x = pre[r * nk:(r + 1) * nk, cb * LANES:(cb + 1) * LANES]
            g = x * (1.0 + jnp.tanh(_GELU_C * (x + 0.044715 * (x * x * x))))
            o_ref[r * nk:(r + 1) * nk, cb * LANES:(cb + 1) * LANES] = (g * w).astype(o_ref.dtype)


def peer_act(xn, u, e1t, e2t, tht, *, tt, te):
    n, d = xn.shape
    ne = u.shape[0]
    rows = te // PEER_NKEYS
    nb = tt // LANES
    return pl.pallas_call(
        _peer_act_kernel,
        grid=(n // tt, ne // te),
        in_specs=[pl.BlockSpec((tt, d), lambda t, e: (t, 0)),
                  pl.BlockSpec((te, d), lambda t, e: (e, 0)),
                  pl.BlockSpec((PEER_HEADS, nb, PEER_NKEYS, LANES), lambda t, e: (0, t, 0, 0)),
                  pl.BlockSpec((PEER_HEADS, nb, rows, LANES), lambda t, e: (0, t, e, 0)),
                  pl.BlockSpec((nb, PEER_HEADS, LANES), lambda t, e: (t, 0, 0))],
        out_specs=pl.BlockSpec((te, tt), lambda t, e: (e, t)),
        out_shape=jax.ShapeDtypeStruct((ne, n), BF16),
        compiler_params=_cparams(("parallel", "arbitrary")),
        name="peer_act",
    )(xn, u, e2t, e1t, tht)


def _final_kernel(h_ref, pt_ref, g_ref, o_ref):
    x = h_ref[...] + pt_ref[...].T
    ms = jnp.mean(x * x, axis=-1, keepdims=True)
    o_ref[...] = x * lax.rsqrt(ms + RMS_EPS) * g_ref[...]


def final_norm(h, peer_t, g, *, col0, tm):
    nrows, d = h.shape
    cb0 = col0 // tm
    return pl.pallas_call(
        _final_kernel,
        grid=(nrows // tm,),
        in_specs=[pl.BlockSpec((tm, d), lambda i: (i, 0)),
                  pl.BlockSpec((d, tm), lambda i: (0, cb0 + i)),
                  pl.BlockSpec((1, d), lambda i: (0, 0))],
        out_specs=pl.BlockSpec((tm, d), lambda i: (i, 0)),
        out_shape=jax.ShapeDtypeStruct((nrows, d), F32),
        compiler_params=_cparams(("parallel",)),
        name="final_norm",
    )(h, peer_t, g.reshape(1, d))


def _tiles(n, n_p, n_s):
    def pick(cands, m=n):
        for c in cands:
            if m % c == 0:
                return c
        raise ValueError(f"no tile for {m}")
    both = math.gcd(n_p, n_s)
    return dict(
        rms_tm=pick((256, 128, 64, 32, 16), both),
        mm_tm=pick((768, 512, 256, 128, 64, 32, 16)),
        mm_tm_p=pick((1024, 512, 256, 128, 64, 32, 16), n_p),
        mm_tm_s=pick((256, 128, 64, 32, 16), n_s),
        peer_tt=pick((768, 512, 256, 128)),
        route_tn=pick((256, 128)),
        final_tm=pick((256, 128), both),
    )


def _alibi_slopes(nheads):
    return [2.0 ** (-8.0 * (i + 1) / nheads) for i in range(nheads)]


def kernel(x_prompt, x_sample, cache_k, cache_v, state_gla, page_table, norm1_gain, w_in, w_a2, b_a, gla_gain,
           lambda_q1, lambda_k1, lambda_q2, lambda_k2, diff_gain, w_o, norm2_gain, peer_wq, peer_keys1,
           peer_keys2, peer_u, peer_v, final_gain):
    depth = w_in.shape[0]
    assert depth == 1
    B, T, D = x_prompt.shape
    DB, TS, _ = x_sample.shape
    n_p, n_s = B * T, DB * TS
    n = n_p + n_s
    gla_dv = D // (2 * GLA_HEADS)
    gla_dk = gla_dv // 2
    diff_dv = D // (2 * DIFF_HEADS)
    diff_dk = diff_dv // 2
    gqk_w = GLA_HEADS * gla_dk
    gv_w = GLA_HEADS * gla_dv
    dqk_w = DIFF_HEADS * 2 * diff_dk
    dv_w = DIFF_HEADS * diff_dv
    g_w = 2 * gqk_w + 2 * gv_w
    tl = _tiles(n, n_p, n_s)
    l = 0
    lam_init = 0.8 - 0.6 * math.exp(-0.3 * l)
    xp = x_prompt.reshape(n_p, D)
    xs = x_sample.reshape(n_s, D)

    wb = w_in[l].astype(BF16)
    w_g = wb[:, :g_w]
    w_d = wb[:, g_w + GLA_LOWRANK:]
    w_ga = jnp.pad(wb[:, g_w:g_w + GLA_LOWRANK], ((0, 0), (0, LANES - GLA_LOWRANK)))
    wa2p = jnp.pad(w_a2[l], ((0, LANES - GLA_LOWRANK), (0, 0)))
    w_ob = w_o[l].astype(BF16)
    f32 = F32
    lam = (jnp.exp(jnp.sum(lambda_q1[l].astype(f32) * lambda_k1[l].astype(f32)))
           - jnp.exp(jnp.sum(lambda_q2[l].astype(f32) * lambda_k2[l].astype(f32))) + lam_init).reshape(1)
    slopes = _alibi_slopes(DIFF_HEADS)

    xn = rms_norm([xp, xs], norm1_gain[l], tl["rms_tm"])
    proj_g = matmul(xn, w_g, tl["mm_tm"], 1024, name="proj_gla")
    proj_d = matmul(xn, w_d, tl["mm_tm"], 1024, name="proj_diff")
    ga = matmul(xn, w_ga, tl["mm_tm"], LANES, name="proj_gate")

    gla_args = (proj_g, ga, wa2p, b_a[l].reshape(1, -1), gla_gain[l].reshape(1, -1))
    mix_p, s_p = gla(*gla_args, None, nbatch=B, seqlen=T, chunk=64, valid=64, row0=0, out_cols=D,
                     out_dtype=BF16, dk=gla_dk, dv=gla_dv, name="gla_prompt")
    og_s, s_s = gla(*gla_args, state_gla[l], nbatch=DB, seqlen=TS, chunk=GLA_SUB, valid=TS, row0=n_p,
                    out_cols=gv_w, out_dtype=F32, dk=gla_dk, dv=gla_dv, name="gla_sample")

    out_scale = 1.0 - lam_init
    dgain = diff_gain[l].reshape(1, -1)
    mix_p = attn_prompt(proj_d, mix_p, jnp.asarray(slopes, F32), lam, dgain, nbatch=B, seqlen=T,
                        tq=min(512, T), col_q=0, col_k=dqk_w, col_v=2 * dqk_w, out_col=gv_w,
                        dk=diff_dk, dv=diff_dv, out_scale=out_scale)
    n_pool = cache_k.shape[1]
    ck = cache_k[l].reshape(n_pool, PAGE_SIZE * DIFF_HEADS * 2, diff_dk)
    od_s = attn_sample(proj_d, ck, cache_v[l], page_table, slopes, lam, dgain, nbatch=DB, tnew=TS, row0=n_p,
                       pg=4, colblk_q=0, colblk_k=1, colblk_v=2, dk=diff_dk, dv=diff_dv, out_scale=out_scale)
    mix_s = jnp.concatenate([og_s, od_s], axis=1).astype(BF16)

    h_p = matmul(mix_p, w_ob, tl["mm_tm_p"], 1024, res=xp, name="out_proj")
    h_s = matmul(mix_s, w_ob, tl["mm_tm_s"], 1024, res=xs, name="out_proj_sample")

    xn2 = rms_norm([h_p, h_s], norm2_gain[l], tl["rms_tm"])
    qt = matmul_nt(peer_wq[l].T.astype(BF16), xn2, PEER_HEADS * PEER_DQ, tl["peer_tt"], name="peer_query")
    e1t, e2t, tht = peer_route(qt, peer_keys1[l], peer_keys2[l], tl["route_tn"])
    act_t = peer_act(xn2, peer_u[l].astype(BF16), e1t, e2t, tht, tt=tl["peer_tt"], te=1024)
    peer_t = matmul_kacc(peer_v[l].T.astype(BF16), act_t, D, tl["peer_tt"], 1024, name="peer_out")

    y_p = final_norm(h_p, peer_t, final_gain, col0=0, tm=tl["final_tm"])
    y_s = final_norm(h_s, peer_t, final_gain, col0=n_p, tm=tl["final_tm"])

    k_p = proj_d[:n_p, dqk_w:2 * dqk_w].reshape(1, B, T, DIFF_HEADS, 2, diff_dk)
    v_p = proj_d[:n_p, 2 * dqk_w:].reshape(1, B, T, DIFF_HEADS, diff_dv)
    k_s = proj_d[n_p:, dqk_w:2 * dqk_w].reshape(1, DB, TS, DIFF_HEADS, 2, diff_dk)
    v_s = proj_d[n_p:, 2 * dqk_w:].reshape(1, DB, TS, DIFF_HEADS, diff_dv)
    return (y_p.reshape(B, T, D), y_s.reshape(DB, TS, D), k_p, v_p, s_p[None], k_s, v_s, s_s[None])
```

```python
import functools
import math

import jax
import jax.numpy as jnp
import numpy as np
from jax import lax
from jax.experimental import pallas as pl
from jax.experimental.pallas import tpu as pltpu

F32 = jnp.float32
BF16 = jnp.bfloat16

GLA_HEADS = 4
GLA_LOWRANK = 16
GLA_GATE_TEMP = 16.0
DIFF_HEADS = 8
PEER_HEADS = 8
PEER_NKEYS = 128
PEER_DQ = 128
PEER_TOPK = 16
PAGE_SIZE = 128
RMS_EPS = 1e-6
LANES = 128
SUBLANES = 8
VMEM_LIMIT = 56 * 1024 * 1024
LOG2E = math.log2(math.e)

NT_DIMS = (((1,), (1,)), ((), ()))
TN_DIMS = (((0,), (0,)), ((), ()))
NEG_INF = float("-inf")


def _cparams(sem):
    return pltpu.CompilerParams(dimension_semantics=sem, vmem_limit_bytes=VMEM_LIMIT)


def _rms_kernel(xa_ref, xb_ref, g_ref, o_ref, *, na):
    def norm(x_ref):
        x = x_ref[...]
        ms = jnp.mean(x * x, axis=-1, keepdims=True)
        o_ref[...] = (x * lax.rsqrt(ms + RMS_EPS) * g_ref[...]).astype(o_ref.dtype)

    i = pl.program_id(0)

    @pl.when(i < na)
    def _():
        norm(xa_ref)

    @pl.when(i >= na)
    def _():
        norm(xb_ref)


def rms_norm(xa, xb, g, tm, out_dtype=BF16):
    d = xa.shape[1]
    na, nb = xa.shape[0] // tm, xb.shape[0] // tm
    return pl.pallas_call(
        functools.partial(_rms_kernel, na=na),
        grid=(na + nb,),
        in_specs=[pl.BlockSpec((tm, d), lambda i: (jnp.minimum(i, na - 1), 0)),
                  pl.BlockSpec((tm, d), lambda i: (jnp.maximum(i - na, 0), 0)),
                  pl.BlockSpec((1, d), lambda i: (0, 0))],
        out_specs=pl.BlockSpec((tm, d), lambda i: (i, 0)),
        out_shape=jax.ShapeDtypeStruct(((na + nb) * tm, d), out_dtype),
        compiler_params=_cparams(("arbitrary",)),
        name="rms_norm",
    )(xa, xb, g.reshape(1, d))


def _mm_wt_kernel(x_ref, wt_ref, o_ref):
    o_ref[...] = lax.dot_general(x_ref[...], wt_ref[...], NT_DIMS, preferred_element_type=F32)


def matmul_wt(x, wt, tm, tn, name="matmul_wt"):
    n, k = x.shape
    m = wt.shape[0]
    return pl.pallas_call(
        _mm_wt_kernel,
        grid=(n // tm, m // tn),
        in_specs=[pl.BlockSpec((tm, k), lambda i, j: (i, 0)),
                  pl.BlockSpec((tn, k), lambda i, j: (j, 0))],
        out_specs=pl.BlockSpec((tm, tn), lambda i, j: (i, j)),
        out_shape=jax.ShapeDtypeStruct((n, m), F32),
        compiler_params=_cparams(("parallel", "parallel")),
        name=name,
    )(x, wt)


def _mm2_res_kernel(x1_ref, x2_ref, w_ref, r_ref, o_ref):
    k1 = x1_ref.shape[1]
    o_ref[...] = (r_ref[...] + jnp.dot(x1_ref[...], w_ref[:k1, :], preferred_element_type=F32)
                  + jnp.dot(x2_ref[...], w_ref[k1:, :], preferred_element_type=F32))


def matmul2_res(x1, x2, w, res, tm, tn, name="matmul2_res"):
    n, k1 = x1.shape
    k2 = x2.shape[1]
    m = w.shape[1]
    return pl.pallas_call(
        _mm2_res_kernel,
        grid=(n // tm, m // tn),
        in_specs=[pl.BlockSpec((tm, k1), lambda i, j: (i, 0)),
                  pl.BlockSpec((tm, k2), lambda i, j: (i, 0)),
                  pl.BlockSpec((k1 + k2, tn), lambda i, j: (0, j)),
                  pl.BlockSpec((tm, tn), lambda i, j: (i, j))],
        out_specs=pl.BlockSpec((tm, tn), lambda i, j: (i, j)),
        out_shape=jax.ShapeDtypeStruct((n, m), F32),
        compiler_params=_cparams(("parallel", "parallel")),
        name=name,
    )(x1, x2, w, res)


def _mm_nt_kernel(a_ref, b_ref, o_ref):
    o_ref[...] = lax.dot_general(a_ref[...], b_ref[...], NT_DIMS, preferred_element_type=F32)


def matmul_nt(a, b, tm, tn, name="matmul_nt"):
    m, k = a.shape
    n = b.shape[0]
    return pl.pallas_call(
        _mm_nt_kernel,
        grid=(n // tn, m // tm),
        in_specs=[pl.BlockSpec((tm, k), lambda j, i: (i, 0)),
                  pl.BlockSpec((tn, k), lambda j, i: (j, 0))],
        out_specs=pl.BlockSpec((tm, tn), lambda j, i: (i, j)),
        out_shape=jax.ShapeDtypeStruct((m, n), F32),
        compiler_params=_cparams(("parallel", "parallel")),
        name=name,
    )(a, b)


def _mm_acc_kernel(a_ref, b_ref, o_ref):
    @pl.when(pl.program_id(2) == 0)
    def _():
        o_ref[...] = jnp.zeros_like(o_ref)

    tm = o_ref.shape[0]
    rc = min(tm, 1024)
    for r in range(tm // rc):
        o_ref[r * rc:(r + 1) * rc, :] += jnp.dot(a_ref[r * rc:(r + 1) * rc, :], b_ref[...],
                                                 preferred_element_type=F32)


def matmul_kacc(a, b, tm, tn, tk, name="matmul_kacc"):
    m, k = a.shape
    n = b.shape[1]
    return pl.pallas_call(
        _mm_acc_kernel,
        grid=(m // tm, n // tn, k // tk),
        in_specs=[pl.BlockSpec((tm, tk), lambda i, j, kk: (i, kk)),
                  pl.BlockSpec((tk, tn), lambda i, j, kk: (kk, j))],
        out_specs=pl.BlockSpec((tm, tn), lambda i, j, kk: (i, j)),
        out_shape=jax.ShapeDtypeStruct((m, n), F32),
        compiler_params=_cparams(("parallel", "parallel", "arbitrary")),
        name=name,
    )(a, b)


GLA_SUB = 16


def _log_sigmoid(x):
    return jnp.minimum(x, 0.0) - jnp.log1p(jnp.exp(-jnp.abs(x)))


def _gla_chunk(q, k, v, r, ga, wa, ba, gain, sts, *, chunk, valid, dk, dv):
    nh = GLA_HEADS
    hk = [slice(h * dk, (h + 1) * dk) for h in range(nh)]
    hv = [slice(h * dv, (h + 1) * dv) for h in range(nh)]
    q = q * (dk ** -0.5)
    pre = jnp.dot(ga, wa, preferred_element_type=F32, precision=lax.Precision.HIGHEST) + ba
    g = _log_sigmoid(pre) * (1.0 / GLA_GATE_TEMP)
    if valid < chunk:
        g = jnp.where(lax.broadcasted_iota(jnp.int32, (chunk, 1), 0) < valid, g, 0.0)
    ri = lax.broadcasted_iota(jnp.int32, (chunk, chunk), 0)
    ci = lax.broadcasted_iota(jnp.int32, (chunk, chunk), 1)
    tril = (ci <= ri).astype(F32)
    G = jnp.dot(tril, g, preferred_element_type=F32, precision=lax.Precision.HIGHEST)
    g_last = G[chunk - 1:chunk, :]

    qg = (q * jnp.exp(G)).astype(BF16)
    o = [lax.dot_general(qg[:, hk[h]], sts[h].astype(BF16), NT_DIMS, preferred_element_type=F32)
         for h in range(nh)]

    sub = min(GLA_SUB, chunk)
    lane = lax.broadcasted_iota(jnp.int32, (sub, chunk), 1)
    srow = lax.broadcasted_iota(jnp.int32, (sub, chunk), 0)
    a_rows = [[] for _ in range(nh)]
    for i in range(chunk // sub):
        lo = i * sub
        Gi = G[lo:lo + sub, :]
        qi = q[lo:lo + sub, :]
        if i > 0:
            ref_i = G[lo - 1:lo, :]
            qs = (qi * jnp.exp(Gi - ref_i)).astype(BF16)
            ks = (k * jnp.exp(jnp.minimum(ref_i - G, 0.0))).astype(BF16)
            a_i = [jnp.where(lane < lo, lax.dot_general(qs[:, hk[h]], ks[:, hk[h]], NT_DIMS,
                                                        preferred_element_type=F32), 0.0) for h in range(nh)]
        else:
            a_i = [jnp.zeros((sub, chunk), F32) for _ in range(nh)]
        for s in range(sub):
            w = qi * jnp.exp(jnp.minimum(Gi - G[lo + s:lo + s + 1, :], 0.0)) * k[lo + s:lo + s + 1, :]
            for h in range(nh):
                col = jnp.sum(w[:, hk[h]], axis=1, keepdims=True)
                a_i[h] = jnp.where(lane == lo + s, col, a_i[h])
        for h in range(nh):
            a_rows[h].append(jnp.where(lane <= lo + srow, a_i[h], 0.0))

    kd = (k * jnp.exp(g_last - G)).astype(BF16)
    decay = jnp.exp(g_last)
    vb = v.astype(BF16)
    rg = r * jax.nn.sigmoid(r)
    outs, sts_new = [], []
    for h in range(nh):
        a_mat = a_rows[h][0] if len(a_rows[h]) == 1 else jnp.concatenate(a_rows[h], axis=0)
        oh = o[h] + jnp.dot(a_mat.astype(BF16), vb[:, hv[h]], preferred_element_type=F32)
        upd = lax.dot_general(vb[:, hv[h]], kd[:, hk[h]], TN_DIMS, preferred_element_type=F32)
        sts_new.append(sts[h] * decay[:, hk[h]] + upd)
        oh = oh[:valid, :]
        ms = jnp.mean(oh * oh, axis=-1, keepdims=True)
        outs.append(oh * lax.rsqrt(ms + RMS_EPS) * gain[:, hv[h]] * rg[:, hv[h]])
    return outs, sts_new


def _gla_kernel(*refs, chunk, valid, has_s0, dk, dv):
    if has_s0:
        (q_ref, k_ref, v_ref, r_ref, ga_ref, wa_ref, ba_ref, gain_ref, s0_ref, o_ref, s_ref, st_scr) = refs
    else:
        (q_ref, k_ref, v_ref, r_ref, ga_ref, wa_ref, ba_ref, gain_ref, o_ref, s_ref, st_scr) = refs
        s0_ref = None
    c = pl.program_id(1)
    nc = pl.num_programs(1)

    @pl.when(c == 0)
    def _():
        for h in range(GLA_HEADS):
            if has_s0:
                st_scr[h] = s0_ref[h].T
            else:
                st_scr[h] = jnp.zeros((dv, dk), F32)

    def rows(x):
        if valid < chunk:
            x = jnp.concatenate([x, jnp.zeros((chunk - valid, x.shape[1]), x.dtype)], axis=0)
        return x

    outs, sts_new = _gla_chunk(rows(q_ref[...]), rows(k_ref[...]), rows(v_ref[...]), r_ref[...],
                               rows(ga_ref[...]), wa_ref[...], ba_ref[...], gain_ref[...],
                               [st_scr[h] for h in range(GLA_HEADS)], chunk=chunk, valid=valid, dk=dk, dv=dv)
    for h in range(GLA_HEADS):
        st_scr[h] = sts_new[h]
        o_ref[:, h * dv:(h + 1) * dv] = outs[h].astype(o_ref.dtype)

    @pl.when(c == nc - 1)
    def _():
        for h in range(GLA_HEADS):
            s_ref[h] = sts_new[h].T


def gla(p, ga, wa2p, b_a, gain, s0, *, nbatch, seqlen, chunk, valid, row0, out_cols, out_dtype, dk, dv, name):
    nc = seqlen // valid
    rb0 = row0 // valid
    h_n = GLA_HEADS
    qk_w, v_w = h_n * dk, h_n * dv
    assert v_w == 2 * qk_w
    has_s0 = s0 is not None

    def rowblk(b, c):
        return rb0 + b * nc + c

    in_specs = [
        pl.BlockSpec((valid, qk_w), lambda b, c: (rowblk(b, c), 0)),
        pl.BlockSpec((valid, qk_w), lambda b, c: (rowblk(b, c), 1)),
        pl.BlockSpec((valid, v_w), lambda b, c: (rowblk(b, c), 1)),
        pl.BlockSpec((valid, v_w), lambda b, c: (rowblk(b, c), 2)),
        pl.BlockSpec((valid, LANES), lambda b, c: (rowblk(b, c), 0)),
        pl.BlockSpec((LANES, qk_w), lambda b, c: (0, 0)),
        pl.BlockSpec((1, qk_w), lambda b, c: (0, 0)),
        pl.BlockSpec((1, v_w), lambda b, c: (0, 0)),
    ]
    args = [p, p, p, p, ga, wa2p, b_a, gain]
    if has_s0:
        in_specs.append(pl.BlockSpec((None, h_n, dk, dv), lambda b, c: (b, 0, 0, 0)))
        args.append(s0)
    kern = functools.partial(_gla_kernel, chunk=chunk, valid=valid, has_s0=has_s0, dk=dk, dv=dv)
    return pl.pallas_call(
        kern,
        grid=(nbatch, nc),
        in_specs=in_specs,
        out_specs=[pl.BlockSpec((valid, v_w), lambda b, c: (b * nc + c, 0)),
                   pl.BlockSpec((None, h_n, dk, dv), lambda b, c: (b, 0, 0, 0))],
        out_shape=[jax.ShapeDtypeStruct((nbatch * seqlen, out_cols), out_dtype),
                   jax.ShapeDtypeStruct((nbatch, h_n, dk, dv), F32)],
        scratch_shapes=[pltpu.VMEM((h_n, dv, dk), F32)],
        compiler_params=_cparams(("parallel", "arbitrary")),
        name=name,
    )(*args)


def _attn_prompt_kernel(qi_ref, ki_ref, slopes_ref, lam_ref, q_ref, k_ref, v_ref, gain_ref, o_ref,
                        m_scr, l_scr, acc_scr, *, tq, tk, dk, dv, out_scale):
    h = pl.program_id(1)
    step_id = pl.program_id(2)
    qi = qi_ref[step_id]
    ki = ki_ref[step_id]

    @pl.when(ki == 0)
    def _():
        m_scr[...] = jnp.full_like(m_scr, NEG_INF)
        l_scr[...] = jnp.zeros_like(l_scr)
        acc_scr[...] = jnp.zeros_like(acc_scr)

    def step(masked):
        krel = (ki * tk - qi * tq + lax.broadcasted_iota(jnp.int32, (1, tk), 1)).astype(F32)
        kbias = (slopes_ref[h] * LOG2E) * krel
        vb = v_ref[...].astype(BF16)
        if masked:
            keep = (lax.broadcasted_iota(jnp.int32, (tq, tk), 0)
                    >= lax.broadcasted_iota(jnp.int32, (tq, tk), 1))
        for c in range(2):
            qc = (q_ref[:, c * dk:(c + 1) * dk] * (dk ** -0.5 * LOG2E)).astype(BF16)
            kc = k_ref[:, c * dk:(c + 1) * dk].astype(BF16)
            s = lax.dot_general(qc, kc, NT_DIMS, preferred_element_type=F32) + kbias
            if masked:
                s = jnp.where(keep, s, NEG_INF)
            m_prev = m_scr[c]
            m_new = jnp.maximum(m_prev, jnp.max(s, axis=-1, keepdims=True))
            alpha = jnp.exp2(m_prev - m_new)
            p = jnp.exp2(s - jnp.tile(m_new, (1, tk // LANES)))
            l_scr[c] = alpha * l_scr[c] + jnp.sum(p, axis=-1, keepdims=True)
            acc_scr[c] = (jnp.tile(alpha, (1, dv // LANES)) * acc_scr[c]
                          + jnp.dot(p.astype(BF16), vb, preferred_element_type=F32))
            m_scr[c] = m_new

    @pl.when(ki < qi)
    def _():
        step(False)

    @pl.when(ki == qi)
    def _():
        step(True)
        lam = lam_ref[0]
        reps = (1, dv // LANES)
        o = acc_scr[0] / jnp.tile(l_scr[0], reps) - lam * (acc_scr[1] / jnp.tile(l_scr[1], reps))
        ms = jnp.mean(o * o, axis=-1, keepdims=True)
        o = o * lax.rsqrt(ms + RMS_EPS) * gain_ref[...] * out_scale
        o_ref[...] = o.astype(o_ref.dtype)


def attn_prompt(p, slopes, lam, gain, *, nbatch, seqlen, tq, col_q, col_k, col_v, dk, dv, out_scale):
    nq = seqlen // tq
    pairs = [(i, j) for i in range(nq) for j in range(i + 1)]
    qi_tab = jnp.asarray([i for i, _ in pairs], jnp.int32)
    ki_tab = jnp.asarray([j for _, j in pairs], jnp.int32)
    kern = functools.partial(_attn_prompt_kernel, tq=tq, tk=tq, dk=dk, dv=dv, out_scale=out_scale)
    smem = pl.BlockSpec(memory_space=pltpu.SMEM)
    grid_spec = pltpu.PrefetchScalarGridSpec(
        num_scalar_prefetch=2,
        grid=(nbatch, DIFF_HEADS, len(pairs)),
        in_specs=[smem, smem,
                  pl.BlockSpec((tq, 2 * dk), lambda b, h, s, qt, kt: (b * nq + qt[s], col_q // (2 * dk) + h)),
                  pl.BlockSpec((tq, 2 * dk), lambda b, h, s, qt, kt: (b * nq + kt[s], col_k // (2 * dk) + h)),
                  pl.BlockSpec((tq, dv), lambda b, h, s, qt, kt: (b * nq + kt[s], col_v // dv + h)),
                  pl.BlockSpec((1, dv), lambda b, h, s, qt, kt: (0, 0))],
        out_specs=pl.BlockSpec((tq, dv), lambda b, h, s, qt, kt: (b * nq + qt[s], h)),
        scratch_shapes=[pltpu.VMEM((2, tq, LANES), F32), pltpu.VMEM((2, tq, LANES), F32),
                        pltpu.VMEM((2, tq, dv), F32)],
    )
    return pl.pallas_call(
        kern,
        grid_spec=grid_spec,
        out_shape=jax.ShapeDtypeStruct((nbatch * seqlen, DIFF_HEADS * dv), BF16),
        compiler_params=_cparams(("parallel", "parallel", "arbitrary")),
        name="attn_prompt",
    )(qi_tab, ki_tab, slopes, lam, p, p, p, gain)


def _attn_sample_kernel(pt_ref, lam_ref, q_ref, kn_ref, vn_ref, gain_ref, c0n_ref, c0s_ref, scol_ref, *rest,
                        pg, tnew, past_len, dk, dv, out_scale):
    k_refs = rest[:pg]
    v_refs = rest[pg:2 * pg]
    o_ref = rest[2 * pg]
    qt_scr, m_scr, l_scr, acc_scr = rest[2 * pg + 1:]
    j = pl.program_id(1)
    nj = pl.num_programs(1)
    nh = DIFF_HEADS
    half = nh * tnew
    page_lanes = PAGE_SIZE * nh

    @pl.when(j == 0)
    def _():
        s_parts = []
        for c in range(2):
            cols = [slice((2 * h + c) * dk, (2 * h + c + 1) * dk) for h in range(nh)]
            qt = jnp.concatenate([q_ref[:, cs] for cs in cols], axis=0) * (dk ** -0.5 * LOG2E)
            qt_scr[c] = qt.astype(BF16)
            kn = jnp.concatenate([kn_ref[:, cs] for cs in cols], axis=0)
            s_parts.append(lax.dot_general(qt, kn, NT_DIMS, preferred_element_type=F32))
        s = jnp.concatenate(s_parts, axis=0) - c0n_ref[...]
        m = jnp.max(s, axis=-1, keepdims=True)
        p = jnp.exp2(s - m)
        vn = jnp.concatenate([vn_ref[:, h * dv:(h + 1) * dv] for h in range(nh)], axis=0)
        m_scr[...] = m
        l_scr[...] = jnp.sum(p, axis=-1, keepdims=True)
        acc_scr[...] = jnp.dot(p, vn, preferred_element_type=F32)

    s_parts = []
    for c in range(2):
        qt = qt_scr[c]
        s_parts.append(jnp.concatenate(
            [lax.dot_general(qt, k_refs[i][pl.ds(c, page_lanes, stride=2), :].astype(BF16), NT_DIMS,
                             preferred_element_type=F32) for i in range(pg)], axis=-1))
    s = jnp.concatenate(s_parts, axis=0) - c0s_ref[...]
    addcol = scol_ref[...] * (j * (pg * PAGE_SIZE) - past_len).astype(F32)
    m_prev = m_scr[...]
    m_new = jnp.maximum(m_prev, jnp.max(s, axis=-1, keepdims=True) + addcol)
    alpha = jnp.exp2(m_prev - m_new)
    p = jnp.exp2(s + (addcol - m_new))
    l_scr[...] = alpha * l_scr[...] + jnp.sum(p, axis=-1, keepdims=True)
    pb = p.astype(BF16)
    acc = alpha * acc_scr[...]
    for i in range(pg):
        vi = v_refs[i][...].reshape(page_lanes, dv).astype(BF16)
        acc = acc + jnp.dot(pb[:, i * page_lanes:(i + 1) * page_lanes], vi, preferred_element_type=F32)
    acc_scr[...] = acc
    m_scr[...] = m_new

    @pl.when(j == nj - 1)
    def _():
        lam = lam_ref[0]
        on = acc / l_scr[...]
        for h in range(nh):
            o = on[h * tnew:(h + 1) * tnew, :] - lam * on[half + h * tnew:half + (h + 1) * tnew, :]
            ms = jnp.mean(o * o, axis=-1, keepdims=True)
            o_ref[:, h * dv:(h + 1) * dv] = o * lax.rsqrt(ms + RMS_EPS) * gain_ref[...] * out_scale


def _sample_bias_tiles(slopes, tnew, pg):
    nh = DIFF_HEADS
    r = np.arange(2 * nh * tnew)
    rh, rq = (r // tnew) % nh, r % tnew
    sl = np.asarray(slopes, np.float64)[rh] * LOG2E
    ln = np.arange(nh * tnew)
    lh, lt = ln // tnew, ln % tnew
    ok = (lh[None, :] == rh[:, None]) & (lt[None, :] <= rq[:, None])
    c0n = np.where(ok, sl[:, None] * (rq[:, None] - lt[None, :]), np.inf)
    ls = np.arange(pg * PAGE_SIZE * nh)
    lh, lt = ls % nh, ls // nh
    ok = lh[None, :] == rh[:, None]
    c0s = np.where(ok, sl[:, None] * (rq[:, None] - lt[None, :]), np.inf)
    return (jnp.asarray(c0n, F32), jnp.asarray(c0s, F32), jnp.asarray(sl[:, None], F32))


def attn_sample(p, cache_k, cache_v, page_table, slopes, lam, gain, *, nbatch, tnew, row0, pg,
                colblk_q, colblk_k, colblk_v, dk, dv, out_scale):
    n_pages = page_table.shape[1]
    nj = n_pages // pg
    nh = DIFF_HEADS
    width = nh * dv
    rb0 = row0 // tnew
    rows = 2 * nh * tnew
    c0n, c0s, scol = _sample_bias_tiles(slopes, tnew, pg)
    kern = functools.partial(_attn_sample_kernel, pg=pg, tnew=tnew, past_len=n_pages * PAGE_SIZE,
                             dk=dk, dv=dv, out_scale=out_scale)
    smem = pl.BlockSpec(memory_space=pltpu.SMEM)

    def kpage_spec(i):
        return pl.BlockSpec((None, PAGE_SIZE * nh * 2, dk), lambda b, j, pt: (pt[b, j * pg + i], 0, 0))

    def vpage_spec(i):
        return pl.BlockSpec((None, PAGE_SIZE, nh, dv), lambda b, j, pt: (pt[b, j * pg + i], 0, 0, 0))

    in_specs = [smem,
                pl.BlockSpec((tnew, width), lambda b, j, pt: (rb0 + b, colblk_q)),
                pl.BlockSpec((tnew, width), lambda b, j, pt: (rb0 + b, colblk_k)),
                pl.BlockSpec((tnew, width), lambda b, j, pt: (rb0 + b, colblk_v)),
                pl.BlockSpec((1, dv), lambda b, j, pt: (0, 0)),
                pl.BlockSpec(c0n.shape, lambda b, j, pt: (0, 0)),
                pl.BlockSpec(c0s.shape, lambda b, j, pt: (0, 0)),
                pl.BlockSpec((rows, 1), lambda b, j, pt: (0, 0))]
    in_specs += [kpage_spec(i) for i in range(pg)] + [vpage_spec(i) for i in range(pg)]
    grid_spec = pltpu.PrefetchScalarGridSpec(
        num_scalar_prefetch=1,
        grid=(nbatch, nj),
        in_specs=in_specs,
        out_specs=pl.BlockSpec((tnew, width), lambda b, j, pt: (b, 0)),
        scratch_shapes=[pltpu.VMEM((2, nh * tnew, dk), BF16),
                        pltpu.VMEM((rows, 1), F32),
                        pltpu.VMEM((rows, 1), F32),
                        pltpu.VMEM((rows, dv), F32)],
    )
    return pl.pallas_call(
        kern,
        grid_spec=grid_spec,
        out_shape=jax.ShapeDtypeStruct((nbatch * tnew, width), F32),
        compiler_params=_cparams(("parallel", "arbitrary")),
        name="attn_sample",
    )(page_table, lam, p, p, p, gain, c0n, c0s, scol, *([cache_k] * pg), *([cache_v] * pg))


N_TOP = PEER_TOPK + 1


def _top_values(x, n):
    cur = jnp.max(x, axis=0, keepdims=True)
    vals = [cur]
    for _ in range(n - 1):
        cur = jnp.max(jnp.where(x < cur, x, NEG_INF), axis=0, keepdims=True)
        vals.append(cur)
    return vals


_CAND_PAIRS = [(a, b) for a in range(N_TOP) for b in range(N_TOP) if (a + 1) * (b + 1) <= N_TOP]
_CAND_ROWS = -(-len(_CAND_PAIRS) // SUBLANES) * SUBLANES


def _route_kernel(qt_ref, k1_ref, k2_ref, e1_ref, e2_ref, th_ref, cand_scr):
    half = PEER_DQ // 2
    tn = qt_ref.shape[1]
    cand_scr[...] = jnp.full_like(cand_scr, NEG_INF)
    for h in range(PEER_HEADS):
        q1 = qt_ref[h * PEER_DQ:h * PEER_DQ + half, :]
        q2 = qt_ref[h * PEER_DQ + half:(h + 1) * PEER_DQ, :]
        s1 = jnp.dot(k1_ref[...], q1, preferred_element_type=F32, precision=lax.Precision.HIGHEST)
        s2 = jnp.dot(k2_ref[...], q2, preferred_element_type=F32, precision=lax.Precision.HIGHEST)
        t1 = _top_values(s1, N_TOP)
        t2 = _top_values(s2, N_TOP)
        for i, (a, b) in enumerate(_CAND_PAIRS):
            cand_scr[i:i + 1, :] = t1[a] + t2[b]
        c = _top_values(cand_scr[...], N_TOP)
        m = c[0]
        z = jnp.zeros_like(m)
        for i in range(PEER_TOPK):
            z = z + jnp.exp(c[i] - m)
        scale = 0.5 / z
        e1 = jnp.exp(s1 - t1[0])
        e2 = jnp.exp(s2 - t2[0]) * scale
        thr = jnp.exp(0.5 * (c[PEER_TOPK - 1] + c[PEER_TOPK]) - m) * scale
        for cb in range(tn // LANES):
            e1_ref[h, cb] = e1[:, cb * LANES:(cb + 1) * LANES]
            e2_ref[h, cb] = e2[:, cb * LANES:(cb + 1) * LANES]
            th_ref[cb, h:h + 1, :] = thr[:, cb * LANES:(cb + 1) * LANES]


def peer_route(qt, keys1, keys2, tn):
    n = qt.shape[1]
    nk = PEER_NKEYS
    nb = tn // LANES
    tile4 = pl.BlockSpec((PEER_HEADS, nb, nk, LANES), lambda i: (0, i, 0, 0))
    shape4 = jax.ShapeDtypeStruct((PEER_HEADS, n // LANES, nk, LANES), F32)
    return pl.pallas_call(
        _route_kernel,
        grid=(n // tn,),
        in_specs=[pl.BlockSpec((PEER_HEADS * PEER_DQ, tn), lambda i: (0, i)),
                  pl.BlockSpec((nk, PEER_DQ // 2), lambda i: (0, 0)),
                  pl.BlockSpec((nk, PEER_DQ // 2), lambda i: (0, 0))],
        out_specs=[tile4, tile4, pl.BlockSpec((nb, PEER_HEADS, LANES), lambda i: (i, 0, 0))],
        out_shape=[shape4, shape4, jax.ShapeDtypeStruct((n // LANES, PEER_HEADS, LANES), F32)],
        scratch_shapes=[pltpu.VMEM((_CAND_ROWS, tn), F32)],
        compiler_params=_cparams(("parallel",)),
        name="peer_route",
    )(qt, keys1, keys2)


_GELU_C = math.sqrt(2.0 / math.pi)


def _peer_act_kernel(x_ref, u_ref, e2_ref, e1_ref, th_ref, o_ref):
    te, tt = o_ref.shape
    nk = PEER_NKEYS
    pre = lax.dot_general(u_ref[...], x_ref[...], NT_DIMS, preferred_element_type=F32)
    for r in range(te // nk):
        for cb in range(tt // LANES):
            w = jnp.zeros((nk, LANES), F32)
            for h in range(PEER_HEADS):
                e1 = jnp.tile(jnp.broadcast_to(e1_ref[h, cb, r:r + 1, :], (SUBLANES, LANES)), (nk // SUBLANES, 1))
                th = jnp.tile(jnp.broadcast_to(th_ref[cb, h:h + 1, :], (SUBLANES, LANES)), (nk // SUBLANES, 1))
                p = e2_ref[h, cb] * e1
                w = w + jnp.where(p > th, p, 0.0)
            x = pre[r * nk:(r + 1) * nk, cb * LANES:(cb + 1) * LANES]
            g = x * (1.0 + jnp.tanh(_GELU_C * (x + 0.044715 * (x * x * x))))
            o_ref[r * nk:(r + 1) * nk, cb * LANES:(cb + 1) * LANES] = (g * w).astype(o_ref.dtype)


def peer_act(xn, u, e1t, e2t, tht, *, tt, te):
    n, d = xn.shape
    ne = u.shape[0]
    rows = te // PEER_NKEYS
    nb = tt // LANES
    return pl.pallas_call(
        _peer_act_kernel,
        grid=(n // tt, ne // te),
        in_specs=[pl.BlockSpec((tt, d), lambda t, e: (t, 0)),
                  pl.BlockSpec((te, d), lambda t, e: (e, 0)),
                  pl.BlockSpec((PEER_HEADS, nb, PEER_NKEYS, LANES), lambda t, e: (0, t, 0, 0)),
                  pl.BlockSpec((PEER_HEADS, nb, rows, LANES), lambda t, e: (0, t, e, 0)),
                  pl.BlockSpec((nb, PEER_HEADS, LANES), lambda t, e: (t, 0, 0))],
        out_specs=pl.BlockSpec((te, tt), lambda t, e: (e, t)),
        out_shape=jax.ShapeDtypeStruct((ne, n), BF16),
        compiler_params=_cparams(("parallel", "arbitrary")),
        name="peer_act",
    )(xn, u, e2t, e1t, tht)


def _final_kernel(h_ref, pt_ref, g_ref, o_ref):
    x = h_ref[...] + pt_ref[...].T
    ms = jnp.mean(x * x, axis=-1, keepdims=True)
    o_ref[...] = x * lax.rsqrt(ms + RMS_EPS) * g_ref[...]


def final_norm(h, peer_t, g, *, col0, tm):
    nrows, d = h.shape
    cb0 = col0 // tm
    return pl.pallas_call(
        _final_kernel,
        grid=(nrows // tm,),
        in_specs=[pl.BlockSpec((tm, d), lambda i: (i, 0)),
                  pl.BlockSpec((d, tm), lambda i: (0, cb0 + i)),
                  pl.BlockSpec((1, d), lambda i: (0, 0))],
        out_specs=pl.BlockSpec((tm, d), lambda i: (i, 0)),
        out_shape=jax.ShapeDtypeStruct((nrows, d), F32),
        compiler_params=_cparams(("parallel",)),
        name="final_norm",
    )(h, peer_t, g.reshape(1, d))


def _tiles(n, n_p, n_s):
    def pick(cands, m=n):
        for c in cands:
            if m % c == 0:
                return c
        raise ValueError(f"no tile for {m}")
    both = math.gcd(n_p, n_s)
    return dict(
        rms_tm=pick((256, 128, 64, 32, 16), both),
        mm_tm=pick((768, 512, 256, 128, 64, 32, 16)),
        mm_tm_p=pick((1024, 512, 256, 128, 64, 32, 16), n_p),
        mm_tm_s=pick((256, 128, 64, 32, 16), n_s),
        peer_tt=pick((768, 512, 256, 128)),
        route_tn=pick((256, 128)),
        final_tm=pick((256, 128), both),
    )


def _alibi_slopes(nheads):
    return [2.0 ** (-8.0 * (i + 1) / nheads) for i in range(nheads)]


def kernel(x_prompt, x_sample, cache_k, cache_v, state_gla, page_table, norm1_gain, w_in, w_a2, b_a, gla_gain,
           lambda_q1, lambda_k1, lambda_q2, lambda_k2, diff_gain, w_o, norm2_gain, peer_wq, peer_keys1,
           peer_keys2, peer_u, peer_v, final_gain):
    depth = w_in.shape[0]
    assert depth == 1
    B, T, D = x_prompt.shape
    DB, TS, _ = x_sample.shape
    n_p, n_s = B * T, DB * TS
    n = n_p + n_s
    gla_dv = D // (2 * GLA_HEADS)
    gla_dk = gla_dv // 2
    diff_dv = D // (2 * DIFF_HEADS)
    diff_dk = diff_dv // 2
    gqk_w = GLA_HEADS * gla_dk
    gv_w = GLA_HEADS * gla_dv
    dqk_w = DIFF_HEADS * 2 * diff_dk
    dv_w = DIFF_HEADS * diff_dv
    g_w = 2 * gqk_w + 2 * gv_w
    tl = _tiles(n, n_p, n_s)
    l = 0
    lam_init = 0.8 - 0.6 * math.exp(-0.3 * l)
    xp = x_prompt.reshape(n_p, D)
    xs = x_sample.reshape(n_s, D)

    wt = w_in[l].T
    wt_g = wt[:g_w].astype(BF16)
    wt_d = wt[g_w + GLA_LOWRANK:].astype(BF16)
    wt_ga = jnp.pad(wt[g_w:g_w + GLA_LOWRANK], ((0, LANES - GLA_LOWRANK), (0, 0))).astype(BF16)
    wa2p = jnp.pad(w_a2[l], ((0, LANES - GLA_LOWRANK), (0, 0)))
    w_ob = w_o[l].astype(BF16)
    f32 = F32
    lam = (jnp.exp(jnp.sum(lambda_q1[l].astype(f32) * lambda_k1[l].astype(f32)))
           - jnp.exp(jnp.sum(lambda_q2[l].astype(f32) * lambda_k2[l].astype(f32))) + lam_init).reshape(1)
    slopes = _alibi_slopes(DIFF_HEADS)

    xn = rms_norm(xp, xs, norm1_gain[l], tl["rms_tm"])
    proj_g = matmul_wt(xn, wt_g, tl["mm_tm"], 1024, name="proj_gla")
    proj_d = matmul_wt(xn, wt_d, tl["mm_tm"], 1024, name="proj_diff")
    ga = matmul_wt(xn, wt_ga, tl["mm_tm"], LANES, name="proj_gate")

    gla_args = (proj_g, ga, wa2p, b_a[l].reshape(1, -1), gla_gain[l].reshape(1, -1))
    og_p, s_p = gla(*gla_args, None, nbatch=B, seqlen=T, chunk=64, valid=64, row0=0, out_cols=gv_w,
                    out_dtype=BF16, dk=gla_dk, dv=gla_dv, name="gla_prompt")
    og_s, s_s = gla(*gla_args, state_gla[l], nbatch=DB, seqlen=TS, chunk=GLA_SUB, valid=TS, row0=n_p,
                    out_cols=gv_w, out_dtype=F32, dk=gla_dk, dv=gla_dv, name="gla_sample")

    out_scale = 1.0 - lam_init
    dgain = diff_gain[l].reshape(1, -1)
    od_p = attn_prompt(proj_d, jnp.asarray(slopes, F32), lam, dgain, nbatch=B, seqlen=T, tq=min(512, T),
                       col_q=0, col_k=dqk_w, col_v=2 * dqk_w, dk=diff_dk, dv=diff_dv, out_scale=out_scale)
    n_pool = cache_k.shape[1]
    ck = cache_k[l].reshape(n_pool, PAGE_SIZE * DIFF_HEADS * 2, diff_dk)
    od_s = attn_sample(proj_d, ck, cache_v[l], page_table, slopes, lam, dgain, nbatch=DB, tnew=TS, row0=n_p,
                       pg=4, colblk_q=0, colblk_k=1, colblk_v=2, dk=diff_dk, dv=diff_dv, out_scale=out_scale)

    h_p = matmul2_res(og_p, od_p, w_ob, xp, tl["mm_tm_p"], 1024, name="out_proj")
    h_s = matmul2_res(og_s.astype(BF16), od_s.astype(BF16), w_ob, xs, tl["mm_tm_s"], 1024,
                      name="out_proj_sample")

    xn2 = rms_norm(h_p, h_s, norm2_gain[l], tl["rms_tm"])
    qt = matmul_nt(peer_wq[l].T.astype(BF16), xn2, PEER_HEADS * PEER_DQ, tl["peer_tt"], name="peer_query")
    e1t, e2t, tht = peer_route(qt, peer_keys1[l], peer_keys2[l], tl["route_tn"])
    act_t = peer_act(xn2, peer_u[l].astype(BF16), e1t, e2t, tht, tt=tl["peer_tt"], te=1024)
    peer_t = matmul_kacc(peer_v[l].T.astype(BF16), act_t, D, tl["peer_tt"], 1024, name="peer_out")

    y_p = final_norm(h_p, peer_t, final_gain, col0=0, tm=tl["final_tm"])
    y_s = final_norm(h_s, peer_t, final_gain, col0=n_p, tm=tl["final_tm"])

    k_p = proj_d[:n_p, dqk_w:2 * dqk_w].reshape(1, B, T, DIFF_HEADS, 2, diff_dk)
    v_p = proj_d[:n_p, 2 * dqk_w:].reshape(1, B, T, DIFF_HEADS, diff_dv)
    k_s = proj_d[n_p:, dqk_w:2 * dqk_w].reshape(1, DB, TS, DIFF_HEADS, 2, diff_dk)
    v_s = proj_d[n_p:, 2 * dqk_w:].reshape(1, DB, TS, DIFF_HEADS, diff_dv)
    return (y_p.reshape(B, T, D), y_s.reshape(DB, TS, D), k_p, v_p, s_p[None], k_s, v_s, s_s[None])
```

```python
import functools
import math

import jax
import jax.numpy as jnp
import numpy as np
from jax import lax
from jax.experimental import pallas as pl
from jax.experimental.pallas import tpu as pltpu

F32 = jnp.float32
BF16 = jnp.bfloat16

GLA_HEADS = 4
GLA_LOWRANK = 16
GLA_GATE_TEMP = 16.0
DIFF_HEADS = 8
PEER_HEADS = 8
PEER_NKEYS = 128
PEER_DQ = 128
PEER_TOPK = 16
PAGE_SIZE = 128
RMS_EPS = 1e-6
LANES = 128
SUBLANES = 8
VMEM_LIMIT = 56 * 1024 * 1024
LOG2E = math.log2(math.e)

NT_DIMS = (((1,), (1,)), ((), ()))
TN_DIMS = (((0,), (0,)), ((), ()))
NEG_INF = float("-inf")


def _cparams(sem):
    return pltpu.CompilerParams(dimension_semantics=sem, vmem_limit_bytes=VMEM_LIMIT)


def _rms_kernel(xa_ref, xb_ref, g_ref, o_ref, *, na):
    def norm(x_ref):
        x = x_ref[...]
        ms = jnp.mean(x * x, axis=-1, keepdims=True)
        o_ref[...] = (x * lax.rsqrt(ms + RMS_EPS) * g_ref[...]).astype(o_ref.dtype)

    i = pl.program_id(0)

    @pl.when(i < na)
    def _():
        norm(xa_ref)

    @pl.when(i >= na)
    def _():
        norm(xb_ref)


def rms_norm(xa, xb, g, tm, out_dtype=BF16):
    d = xa.shape[1]
    na, nb = xa.shape[0] // tm, xb.shape[0] // tm
    return pl.pallas_call(
        functools.partial(_rms_kernel, na=na),
        grid=(na + nb,),
        in_specs=[pl.BlockSpec((tm, d), lambda i: (jnp.minimum(i, na - 1), 0)),
                  pl.BlockSpec((tm, d), lambda i: (jnp.maximum(i - na, 0), 0)),
                  pl.BlockSpec((1, d), lambda i: (0, 0))],
        out_specs=pl.BlockSpec((tm, d), lambda i: (i, 0)),
        out_shape=jax.ShapeDtypeStruct(((na + nb) * tm, d), out_dtype),
        compiler_params=_cparams(("arbitrary",)),
        name="rms_norm",
    )(xa, xb, g.reshape(1, d))


def _mm_wt_kernel(x_ref, wt_ref, o_ref):
    o_ref[...] = lax.dot_general(x_ref[...], wt_ref[...], NT_DIMS, preferred_element_type=F32)


def matmul_wt(x, wt, tm, tn, name="matmul_wt"):
    n, k = x.shape
    m = wt.shape[0]
    return pl.pallas_call(
        _mm_wt_kernel,
        grid=(n // tm, m // tn),
        in_specs=[pl.BlockSpec((tm, k), lambda i, j: (i, 0)),
                  pl.BlockSpec((tn, k), lambda i, j: (j, 0))],
        out_specs=pl.BlockSpec((tm, tn), lambda i, j: (i, j)),
        out_shape=jax.ShapeDtypeStruct((n, m), F32),
        compiler_params=_cparams(("parallel", "parallel")),
        name=name,
    )(x, wt)


def _mm2_res_kernel(x1_ref, x2_ref, w_ref, r_ref, o_ref):
    k1 = x1_ref.shape[1]
    o_ref[...] = (r_ref[...] + jnp.dot(x1_ref[...], w_ref[:k1, :], preferred_element_type=F32)
                  + jnp.dot(x2_ref[...], w_ref[k1:, :], preferred_element_type=F32))


def matmul2_res(x1, x2, w, res, tm, tn, name="matmul2_res"):
    n, k1 = x1.shape
    k2 = x2.shape[1]
    m = w.shape[1]
    return pl.pallas_call(
        _mm2_res_kernel,
        grid=(n // tm, m // tn),
        in_specs=[pl.BlockSpec((tm, k1), lambda i, j: (i, 0)),
                  pl.BlockSpec((tm, k2), lambda i, j: (i, 0)),
                  pl.BlockSpec((k1 + k2, tn), lambda i, j: (0, j)),
                  pl.BlockSpec((tm, tn), lambda i, j: (i, j))],
        out_specs=pl.BlockSpec((tm, tn), lambda i, j: (i, j)),
        out_shape=jax.ShapeDtypeStruct((n, m), F32),
        compiler_params=_cparams(("parallel", "parallel")),
        name=name,
    )(x1, x2, w, res)


def _mm_nt_kernel(a_ref, b_ref, o_ref):
    o_ref[...] = lax.dot_general(a_ref[...], b_ref[...], NT_DIMS, preferred_element_type=F32)


def matmul_nt(a, b, tm, tn, name="matmul_nt"):
    m, k = a.shape
    n = b.shape[0]
    return pl.pallas_call(
        _mm_nt_kernel,
        grid=(n // tn, m // tm),
        in_specs=[pl.BlockSpec((tm, k), lambda j, i: (i, 0)),
                  pl.BlockSpec((tn, k), lambda j, i: (j, 0))],
        out_specs=pl.BlockSpec((tm, tn), lambda j, i: (i, j)),
        out_shape=jax.ShapeDtypeStruct((m, n), F32),
        compiler_params=_cparams(("parallel", "parallel")),
        name=name,
    )(a, b)


def _mm_acc_kernel(a_ref, b_ref, o_ref):
    @pl.when(pl.program_id(2) == 0)
    def _():
        o_ref[...] = jnp.zeros_like(o_ref)

    tm = o_ref.shape[0]
    rc = min(tm, 1024)
    for r in range(tm // rc):
        o_ref[r * rc:(r + 1) * rc, :] += jnp.dot(a_ref[r * rc:(r + 1) * rc, :], b_ref[...],
                                                 preferred_element_type=F32)


def matmul_kacc(a, b, tm, tn, tk, name="matmul_kacc"):
    m, k = a.shape
    n = b.shape[1]
    return pl.pallas_call(
        _mm_acc_kernel,
        grid=(m // tm, n // tn, k // tk),
        in_specs=[pl.BlockSpec((tm, tk), lambda i, j, kk: (i, kk)),
                  pl.BlockSpec((tk, tn), lambda i, j, kk: (kk, j))],
        out_specs=pl.BlockSpec((tm, tn), lambda i, j, kk: (i, j)),
        out_shape=jax.ShapeDtypeStruct((m, n), F32),
        compiler_params=_cparams(("parallel", "parallel", "arbitrary")),
        name=name,
    )(a, b)


GLA_SUB = 16


def _log_sigmoid(x):
    return jnp.minimum(x, 0.0) - jnp.log1p(jnp.exp(-jnp.abs(x)))


def _gla_chunk(q, k, v, r, ga, wa, ba, gain, sts, *, chunk, valid, dk, dv):
    nh = GLA_HEADS
    hk = [slice(h * dk, (h + 1) * dk) for h in range(nh)]
    hv = [slice(h * dv, (h + 1) * dv) for h in range(nh)]
    q = q * (dk ** -0.5)
    pre = jnp.dot(ga, wa, preferred_element_type=F32, precision=lax.Precision.HIGHEST) + ba
    g = _log_sigmoid(pre) * (1.0 / GLA_GATE_TEMP)
    if valid < chunk:
        g = jnp.where(lax.broadcasted_iota(jnp.int32, (chunk, 1), 0) < valid, g, 0.0)
    ri = lax.broadcasted_iota(jnp.int32, (chunk, chunk), 0)
    ci = lax.broadcasted_iota(jnp.int32, (chunk, chunk), 1)
    tril = (ci <= ri).astype(F32)
    G = jnp.dot(tril, g, preferred_element_type=F32, precision=lax.Precision.HIGHEST)
    g_last = G[chunk - 1:chunk, :]

    qg = (q * jnp.exp(G)).astype(BF16)
    o = [lax.dot_general(qg[:, hk[h]], sts[h].astype(BF16), NT_DIMS, preferred_element_type=F32)
         for h in range(nh)]

    sub = min(GLA_SUB, chunk)
    lane = lax.broadcasted_iota(jnp.int32, (sub, chunk), 1)
    srow = lax.broadcasted_iota(jnp.int32, (sub, chunk), 0)
    a_rows = [[] for _ in range(nh)]
    for i in range(chunk // sub):
        lo = i * sub
        Gi = G[lo:lo + sub, :]
        qi = q[lo:lo + sub, :]
        if i > 0:
            ref_i = G[lo - 1:lo, :]
            qs = (qi * jnp.exp(Gi - ref_i)).astype(BF16)
            ks = (k * jnp.exp(jnp.minimum(ref_i - G, 0.0))).astype(BF16)
            a_i = [jnp.where(lane < lo, lax.dot_general(qs[:, hk[h]], ks[:, hk[h]], NT_DIMS,
                                                        preferred_element_type=F32), 0.0) for h in range(nh)]
        else:
            a_i = [jnp.zeros((sub, chunk), F32) for _ in range(nh)]
        for s in range(sub):
            w = qi * jnp.exp(jnp.minimum(Gi - G[lo + s:lo + s + 1, :], 0.0)) * k[lo + s:lo + s + 1, :]
            for h in range(nh):
                col = jnp.sum(w[:, hk[h]], axis=1, keepdims=True)
                a_i[h] = jnp.where(lane == lo + s, col, a_i[h])
        for h in range(nh):
            a_rows[h].append(jnp.where(lane <= lo + srow, a_i[h], 0.0))

    kd = (k * jnp.exp(g_last - G)).astype(BF16)
    decay = jnp.exp(g_last)
    vb = v.astype(BF16)
    rg = r * jax.nn.sigmoid(r)
    outs, sts_new = [], []
    for h in range(nh):
        a_mat = a_rows[h][0] if len(a_rows[h]) == 1 else jnp.concatenate(a_rows[h], axis=0)
        oh = o[h] + jnp.dot(a_mat.astype(BF16), vb[:, hv[h]], preferred_element_type=F32)
        upd = lax.dot_general(vb[:, hv[h]], kd[:, hk[h]], TN_DIMS, preferred_element_type=F32)
        sts_new.append(sts[h] * decay[:, hk[h]] + upd)
        oh = oh[:valid, :]
        ms = jnp.mean(oh * oh, axis=-1, keepdims=True)
        outs.append(oh * lax.rsqrt(ms + RMS_EPS) * gain[:, hv[h]] * rg[:, hv[h]])
    return outs, sts_new


def _gla_kernel(*refs, chunk, valid, has_s0, dk, dv):
    if has_s0:
        (q_ref, k_ref, v_ref, r_ref, ga_ref, wa_ref, ba_ref, gain_ref, s0_ref, o_ref, s_ref, st_scr) = refs
    else:
        (q_ref, k_ref, v_ref, r_ref, ga_ref, wa_ref, ba_ref, gain_ref, o_ref, s_ref, st_scr) = refs
        s0_ref = None
    c = pl.program_id(1)
    nc = pl.num_programs(1)

    @pl.when(c == 0)
    def _():
        for h in range(GLA_HEADS):
            if has_s0:
                st_scr[h] = s0_ref[h].T
            else:
                st_scr[h] = jnp.zeros((dv, dk), F32)

    def rows(x):
        if valid < chunk:
            x = jnp.concatenate([x, jnp.zeros((chunk - valid, x.shape[1]), x.dtype)], axis=0)
        return x

    outs, sts_new = _gla_chunk(rows(q_ref[...]), rows(k_ref[...]), rows(v_ref[...]), r_ref[...],
                               rows(ga_ref[...]), wa_ref[...], ba_ref[...], gain_ref[...],
                               [st_scr[h] for h in range(GLA_HEADS)], chunk=chunk, valid=valid, dk=dk, dv=dv)
    for h in range(GLA_HEADS):
        st_scr[h] = sts_new[h]
        o_ref[:, h * dv:(h + 1) * dv] = outs[h].astype(o_ref.dtype)

    @pl.when(c == nc - 1)
    def _():
        for h in range(GLA_HEADS):
            s_ref[h] = sts_new[h].T


def gla(p, ga, wa2p, b_a, gain, s0, *, nbatch, seqlen, chunk, valid, row0, out_cols, out_dtype, dk, dv, name):
    nc = seqlen // valid
    rb0 = row0 // valid
    h_n = GLA_HEADS
    qk_w, v_w = h_n * dk, h_n * dv
    assert v_w == 2 * qk_w
    has_s0 = s0 is not None

    def rowblk(b, c):
        return rb0 + b * nc + c

    in_specs = [
        pl.BlockSpec((valid, qk_w), lambda b, c: (rowblk(b, c), 0)),
        pl.BlockSpec((valid, qk_w), lambda b, c: (rowblk(b, c), 1)),
        pl.BlockSpec((valid, v_w), lambda b, c: (rowblk(b, c), 1)),
        pl.BlockSpec((valid, v_w), lambda b, c: (rowblk(b, c), 2)),
        pl.BlockSpec((valid, LANES), lambda b, c: (rowblk(b, c), 0)),
        pl.BlockSpec((LANES, qk_w), lambda b, c: (0, 0)),
        pl.BlockSpec((1, qk_w), lambda b, c: (0, 0)),
        pl.BlockSpec((1, v_w), lambda b, c: (0, 0)),
    ]
    args = [p, p, p, p, ga, wa2p, b_a, gain]
    if has_s0:
        in_specs.append(pl.BlockSpec((None, h_n, dk, dv), lambda b, c: (b, 0, 0, 0)))
        args.append(s0)
    kern = functools.partial(_gla_kernel, chunk=chunk, valid=valid, has_s0=has_s0, dk=dk, dv=dv)
    return pl.pallas_call(
        kern,
        grid=(nbatch, nc),
        in_specs=in_specs,
        out_specs=[pl.BlockSpec((valid, v_w), lambda b, c: (b * nc + c, 0)),
                   pl.BlockSpec((None, h_n, dk, dv), lambda b, c: (b, 0, 0, 0))],
        out_shape=[jax.ShapeDtypeStruct((nbatch * seqlen, out_cols), out_dtype),
                   jax.ShapeDtypeStruct((nbatch, h_n, dk, dv), F32)],
        scratch_shapes=[pltpu.VMEM((h_n, dv, dk), F32)],
        compiler_params=_cparams(("parallel", "arbitrary")),
        name=name,
    )(*args)


def _attn_prompt_kernel(qi_ref, ki_ref, slopes_ref, lam_ref, q_ref, k_ref, v_ref, gain_ref, o_ref,
                        m_scr, l_scr, acc_scr, *, tq, tk, dk, dv, out_scale):
    h = pl.program_id(1)
    step_id = pl.program_id(2)
    qi = qi_ref[step_id]
    ki = ki_ref[step_id]

    @pl.when(ki == 0)
    def _():
        m_scr[...] = jnp.full_like(m_scr, NEG_INF)
        l_scr[...] = jnp.zeros_like(l_scr)
        acc_scr[...] = jnp.zeros_like(acc_scr)

    def step(masked):
        krel = (ki * tk - qi * tq + lax.broadcasted_iota(jnp.int32, (1, tk), 1)).astype(F32)
        kbias = (slopes_ref[h] * LOG2E) * krel
        vb = v_ref[...].astype(BF16)
        if masked:
            keep = (lax.broadcasted_iota(jnp.int32, (tq, tk), 0)
                    >= lax.broadcasted_iota(jnp.int32, (tq, tk), 1))
        for c in range(2):
            qc = (q_ref[:, c * dk:(c + 1) * dk] * (dk ** -0.5 * LOG2E)).astype(BF16)
            kc = k_ref[:, c * dk:(c + 1) * dk].astype(BF16)
            s = lax.dot_general(qc, kc, NT_DIMS, preferred_element_type=F32) + kbias
            if masked:
                s = jnp.where(keep, s, NEG_INF)
            m_prev = m_scr[c]
            m_new = jnp.maximum(m_prev, jnp.max(s, axis=-1, keepdims=True))
            alpha = jnp.exp2(m_prev - m_new)
            p = jnp.exp2(s - jnp.tile(m_new, (1, tk // LANES)))
            l_scr[c] = alpha * l_scr[c] + jnp.sum(p, axis=-1, keepdims=True)
            acc_scr[c] = (jnp.tile(alpha, (1, dv // LANES)) * acc_scr[c]
                          + jnp.dot(p.astype(BF16), vb, preferred_element_type=F32))
            m_scr[c] = m_new

    @pl.when(ki < qi)
    def _():
        step(False)

    @pl.when(ki == qi)
    def _():
        step(True)
        lam = lam_ref[0]
        reps = (1, dv // LANES)
        o = acc_scr[0] / jnp.tile(l_scr[0], reps) - lam * (acc_scr[1] / jnp.tile(l_scr[1], reps))
        ms = jnp.mean(o * o, axis=-1, keepdims=True)
        o = o * lax.rsqrt(ms + RMS_EPS) * gain_ref[...] * out_scale
        o_ref[...] = o.astype(o_ref.dtype)


def attn_prompt(p, slopes, lam, gain, *, nbatch, seqlen, tq, col_q, col_k, col_v, dk, dv, out_scale):
    nq = seqlen // tq
    pairs = [(i, j) for i in range(nq) for j in range(i + 1)]
    qi_tab = jnp.asarray([i for i, _ in pairs], jnp.int32)
    ki_tab = jnp.asarray([j for _, j in pairs], jnp.int32)
    kern = functools.partial(_attn_prompt_kernel, tq=tq, tk=tq, dk=dk, dv=dv, out_scale=out_scale)
    smem = pl.BlockSpec(memory_space=pltpu.SMEM)
    grid_spec = pltpu.PrefetchScalarGridSpec(
        num_scalar_prefetch=2,
        grid=(nbatch, DIFF_HEADS, len(pairs)),
        in_specs=[smem, smem,
                  pl.BlockSpec((tq, 2 * dk), lambda b, h, s, qt, kt: (b * nq + qt[s], col_q // (2 * dk) + h)),
                  pl.BlockSpec((tq, 2 * dk), lambda b, h, s, qt, kt: (b * nq + kt[s], col_k // (2 * dk) + h)),
                  pl.BlockSpec((tq, dv), lambda b, h, s, qt, kt: (b * nq + kt[s], col_v // dv + h)),
                  pl.BlockSpec((1, dv), lambda b, h, s, qt, kt: (0, 0))],
        out_specs=pl.BlockSpec((tq, dv), lambda b, h, s, qt, kt: (b * nq + qt[s], h)),
        scratch_shapes=[pltpu.VMEM((2, tq, LANES), F32), pltpu.VMEM((2, tq, LANES), F32),
                        pltpu.VMEM((2, tq, dv), F32)],
    )
    return pl.pallas_call(
        kern,
        grid_spec=grid_spec,
        out_shape=jax.ShapeDtypeStruct((nbatch * seqlen, DIFF_HEADS * dv), BF16),
        compiler_params=_cparams(("parallel", "parallel", "arbitrary")),
        name="attn_prompt",
    )(qi_tab, ki_tab, slopes, lam, p, p, p, gain)


def _attn_sample_kernel(pt_ref, lam_ref, q_ref, kn_ref, vn_ref, gain_ref, c0n_ref, c0s_ref, scol_ref, *rest,
                        pg, tnew, past_len, dk, dv, out_scale):
    k_refs = rest[:pg]
    v_refs = rest[pg:2 * pg]
    o_ref = rest[2 * pg]
    qt_scr, m_scr, l_scr, acc_scr = rest[2 * pg + 1:]
    j = pl.program_id(1)
    nj = pl.num_programs(1)
    nh = DIFF_HEADS
    half = nh * tnew
    page_lanes = PAGE_SIZE * nh

    @pl.when(j == 0)
    def _():
        s_parts = []
        for c in range(2):
            cols = [slice((2 * h + c) * dk, (2 * h + c + 1) * dk) for h in range(nh)]
            qt = jnp.concatenate([q_ref[:, cs] for cs in cols], axis=0) * (dk ** -0.5 * LOG2E)
            qt_scr[c] = qt.astype(BF16)
            kn = jnp.concatenate([kn_ref[:, cs] for cs in cols], axis=0)
            s_parts.append(lax.dot_general(qt, kn, NT_DIMS, preferred_element_type=F32))
        s = jnp.concatenate(s_parts, axis=0) - c0n_ref[...]
        m = jnp.max(s, axis=-1, keepdims=True)
        p = jnp.exp2(s - m)
        vn = jnp.concatenate([vn_ref[:, h * dv:(h + 1) * dv] for h in range(nh)], axis=0)
        m_scr[...] = m
        l_scr[...] = jnp.sum(p, axis=-1, keepdims=True)
        acc_scr[...] = jnp.dot(p, vn, preferred_element_type=F32)

    s_parts = []
    for c in range(2):
        qt = qt_scr[c]
        s_parts.append(jnp.concatenate(
            [lax.dot_general(qt, k_refs[i][pl.ds(c, page_lanes, stride=2), :].astype(BF16), NT_DIMS,
                             preferred_element_type=F32) for i in range(pg)], axis=-1))
    s = jnp.concatenate(s_parts, axis=0) - c0s_ref[...]
    addcol = scol_ref[...] * (j * (pg * PAGE_SIZE) - past_len).astype(F32)
    m_prev = m_scr[...]
    m_new = jnp.maximum(m_prev, jnp.max(s, axis=-1, keepdims=True) + addcol)
    alpha = jnp.exp2(m_prev - m_new)
    p = jnp.exp2(s + (addcol - m_new))
    l_scr[...] = alpha * l_scr[...] + jnp.sum(p, axis=-1, keepdims=True)
    pb = p.astype(BF16)
    acc = alpha * acc_scr[...]
    for i in range(pg):
        vi = v_refs[i][...].reshape(page_lanes, dv).astype(BF16)
        acc = acc + jnp.dot(pb[:, i * page_lanes:(i + 1) * page_lanes], vi, preferred_element_type=F32)
    acc_scr[...] = acc
    m_scr[...] = m_new

    @pl.when(j == nj - 1)
    def _():
        lam = lam_ref[0]
        on = acc / l_scr[...]
        for h in range(nh):
            o = on[h * tnew:(h + 1) * tnew, :] - lam * on[half + h * tnew:half + (h + 1) * tnew, :]
            ms = jnp.mean(o * o, axis=-1, keepdims=True)
            o_ref[:, h * dv:(h + 1) * dv] = o * lax.rsqrt(ms + RMS_EPS) * gain_ref[...] * out_scale


def _sample_bias_tiles(slopes, tnew, pg):
    nh = DIFF_HEADS
    r = np.arange(2 * nh * tnew)
    rh, rq = (r // tnew) % nh, r % tnew
    sl = np.asarray(slopes, np.float64)[rh] * LOG2E
    ln = np.arange(nh * tnew)
    lh, lt = ln // tnew, ln % tnew
    ok = (lh[None, :] == rh[:, None]) & (lt[None, :] <= rq[:, None])
    c0n = np.where(ok, sl[:, None] * (rq[:, None] - lt[None, :]), np.inf)
    ls = np.arange(pg * PAGE_SIZE * nh)
    lh, lt = ls % nh, ls // nh
    ok = lh[None, :] == rh[:, None]
    c0s = np.where(ok, sl[:, None] * (rq[:, None] - lt[None, :]), np.inf)
    return (jnp.asarray(c0n, F32), jnp.asarray(c0s, F32), jnp.asarray(sl[:, None], F32))


def attn_sample(p, cache_k, cache_v, page_table, slopes, lam, gain, *, nbatch, tnew, row0, pg,
                colblk_q, colblk_k, colblk_v, dk, dv, out_scale):
    n_pages = page_table.shape[1]
    nj = n_pages // pg
    nh = DIFF_HEADS
    width = nh * dv
    rb0 = row0 // tnew
    rows = 2 * nh * tnew
    c0n, c0s, scol = _sample_bias_tiles(slopes, tnew, pg)
    kern = functools.partial(_attn_sample_kernel, pg=pg, tnew=tnew, past_len=n_pages * PAGE_SIZE,
                             dk=dk, dv=dv, out_scale=out_scale)
    smem = pl.BlockSpec(memory_space=pltpu.SMEM)

    def kpage_spec(i):
        return pl.BlockSpec((None, PAGE_SIZE * nh * 2, dk), lambda b, j, pt: (pt[b, j * pg + i], 0, 0))

    def vpage_spec(i):
        return pl.BlockSpec((None, PAGE_SIZE, nh, dv), lambda b, j, pt: (pt[b, j * pg + i], 0, 0, 0))

    in_specs = [smem,
                pl.BlockSpec((tnew, width), lambda b, j, pt: (rb0 + b, colblk_q)),
                pl.BlockSpec((tnew, width), lambda b, j, pt: (rb0 + b, colblk_k)),
                pl.BlockSpec((tnew, width), lambda b, j, pt: (rb0 + b, colblk_v)),
                pl.BlockSpec((1, dv), lambda b, j, pt: (0, 0)),
                pl.BlockSpec(c0n.shape, lambda b, j, pt: (0, 0)),
                pl.BlockSpec(c0s.shape, lambda b, j, pt: (0, 0)),
                pl.BlockSpec((rows, 1), lambda b, j, pt: (0, 0))]
    in_specs += [kpage_spec(i) for i in range(pg)] + [vpage_spec(i) for i in range(pg)]
    grid_spec = pltpu.PrefetchScalarGridSpec(
        num_scalar_prefetch=1,
        grid=(nbatch, nj),
        in_specs=in_specs,
        out_specs=pl.BlockSpec((tnew, width), lambda b, j, pt: (b, 0)),
        scratch_shapes=[pltpu.VMEM((2, nh * tnew, dk), BF16),
                        pltpu.VMEM((rows, 1), F32),
                        pltpu.VMEM((rows, 1), F32),
                        pltpu.VMEM((rows, dv), F32)],
    )
    return pl.pallas_call(
        kern,
        grid_spec=grid_spec,
        out_shape=jax.ShapeDtypeStruct((nbatch * tnew, width), F32),
        compiler_params=_cparams(("parallel", "arbitrary")),
        name="attn_sample",
    )(page_table, lam, p, p, p, gain, c0n, c0s, scol, *([cache_k] * pg), *([cache_v] * pg))


N_TOP = PEER_TOPK + 1


def _top_values(x, n):
    cur = jnp.max(x, axis=0, keepdims=True)
    vals = [cur]
    for _ in range(n - 1):
        cur = jnp.max(jnp.where(x < cur, x, NEG_INF), axis=0, keepdims=True)
        vals.append(cur)
    return vals


_CAND_PAIRS = [(a, b) for a in range(N_TOP) for b in range(N_TOP) if (a + 1) * (b + 1) <= N_TOP]
_CAND_ROWS = -(-len(_CAND_PAIRS) // SUBLANES) * SUBLANES


def _route_kernel(qt_ref, k1_ref, k2_ref, e1_ref, e2_ref, th_ref, cand_scr):
    half = PEER_DQ // 2
    tn = qt_ref.shape[1]
    cand_scr[...] = jnp.full_like(cand_scr, NEG_INF)
    for h in range(PEER_HEADS):
        q1 = qt_ref[h * PEER_DQ:h * PEER_DQ + half, :]
        q2 = qt_ref[h * PEER_DQ + half:(h + 1) * PEER_DQ, :]
        s1 = jnp.dot(k1_ref[...], q1, preferred_element_type=F32, precision=lax.Precision.HIGHEST)
        s2 = jnp.dot(k2_ref[...], q2, preferred_element_type=F32, precision=lax.Precision.HIGHEST)
        t1 = _top_values(s1, N_TOP)
        t2 = _top_values(s2, N_TOP)
        for i, (a, b) in enumerate(_CAND_PAIRS):
            cand_scr[i:i + 1, :] = t1[a] + t2[b]
        c = _top_values(cand_scr[...], N_TOP)
        m = c[0]
        z = jnp.zeros_like(m)
        for i in range(PEER_TOPK):
            z = z + jnp.exp(c[i] - m)
        scale = 0.5 / z
        e1 = jnp.exp(s1 - t1[0])
        e2 = jnp.exp(s2 - t2[0]) * scale
        thr = jnp.exp(0.5 * (c[PEER_TOPK - 1] + c[PEER_TOPK]) - m) * scale
        for cb in range(tn // LANES):
            e1_ref[h, cb] = e1[:, cb * LANES:(cb + 1) * LANES]
            e2_ref[h, cb] = e2[:, cb * LANES:(cb + 1) * LANES]
            th_ref[cb, h:h + 1, :] = thr[:, cb * LANES:(cb + 1) * LANES]


def peer_route(qt, keys1, keys2, tn):
    n = qt.shape[1]
    nk = PEER_NKEYS
    nb = tn // LANES
    tile4 = pl.BlockSpec((PEER_HEADS, nb, nk, LANES), lambda i: (0, i, 0, 0))
    shape4 = jax.ShapeDtypeStruct((PEER_HEADS, n // LANES, nk, LANES), F32)
    return pl.pallas_call(
        _route_kernel,
        grid=(n // tn,),
        in_specs=[pl.BlockSpec((PEER_HEADS * PEER_DQ, tn), lambda i: (0, i)),
                  pl.BlockSpec((nk, PEER_DQ // 2), lambda i: (0, 0)),
                  pl.BlockSpec((nk, PEER_DQ // 2), lambda i: (0, 0))],
        out_specs=[tile4, tile4, pl.BlockSpec((nb, PEER_HEADS, LANES), lambda i: (i, 0, 0))],
        out_shape=[shape4, shape4, jax.ShapeDtypeStruct((n // LANES, PEER_HEADS, LANES), F32)],
        scratch_shapes=[pltpu.VMEM((_CAND_ROWS, tn), F32)],
        compiler_params=_cparams(("parallel",)),
        name="peer_route",
    )(qt, keys1, keys2)


_GELU_C = math.sqrt(2.0 / math.pi)


def _peer_act_kernel(x_ref, u_ref, e2_ref, e1_ref, th_ref, o_ref):
    te, tt = o_ref.shape
    nk = PEER_NKEYS
    pre = lax.dot_general(u_ref[...], x_ref[...], NT_DIMS, preferred_element_type=F32)
    for r in range(te // nk):
        for cb in range(tt // LANES):
            w = jnp.zeros((nk, LANES), F32)
            for h in range(PEER_HEADS):
                e1 = jnp.tile(jnp.broadcast_to(e1_ref[h, cb, r:r + 1, :], (SUBLANES, LANES)), (nk // SUBLANES, 1))
                th = jnp.tile(jnp.broadcast_to(th_ref[cb, h:h + 1, :], (SUBLANES, LANES)), (nk // SUBLANES, 1))
                p = e2_ref[h, cb] * e1
                w = w + jnp.where(p > th, p, 0.0)
            x = pre[r * nk:(r + 1) * nk, cb * LANES:(cb + 1) * LANES]
            g = x * (1.0 + jnp.tanh(_GELU_C * (x + 0.044715 * (x * x * x))))
            o_ref[r * nk:(r + 1) * nk, cb * LANES:(cb + 1) * LANES] = (g * w).astype(o_ref.dtype)


def peer_act(xn, u, e1t, e2t, tht, *, tt, te):
    n, d = xn.shape
    ne = u.shape[0]
    rows = te // PEER_NKEYS
    nb = tt // LANES
    return pl.pallas_call(
        _peer_act_kernel,
        grid=(n // tt, ne // te),
        in_specs=[pl.BlockSpec((tt, d), lambda t, e: (t, 0)),
                  pl.BlockSpec((te, d), lambda t, e: (e, 0)),
                  pl.BlockSpec((PEER_HEADS, nb, PEER_NKEYS, LANES), lambda t, e: (0, t, 0, 0)),
                  pl.BlockSpec((PEER_HEADS, nb, rows, LANES), lambda t, e: (0, t, e, 0)),
                  pl.BlockSpec((nb, PEER_HEADS, LANES), lambda t, e: (t, 0, 0))],
        out_specs=pl.BlockSpec((te, tt), lambda t, e: (e, t)),
        out_shape=jax.ShapeDtypeStruct((ne, n), BF16),
        compiler_params=_cparams(("parallel", "arbitrary")),
        name="peer_act",
    )(xn, u, e2t, e1t, tht)


def _final_kernel(h_ref, pt_ref, g_ref, o_ref):
    x = h_ref[...] + pt_ref[...].T
    ms = jnp.mean(x * x, axis=-1, keepdims=True)
    o_ref[...] = x * lax.rsqrt(ms + RMS_EPS) * g_ref[...]


def final_norm(h, peer_t, g, *, col0, tm):
    nrows, d = h.shape
    cb0 = col0 // tm
    return pl.pallas_call(
        _final_kernel,
        grid=(nrows // tm,),
        in_specs=[pl.BlockSpec((tm, d), lambda i: (i, 0)),
                  pl.BlockSpec((d, tm), lambda i: (0, cb0 + i)),
                  pl.BlockSpec((1, d), lambda i: (0, 0))],
        out_specs=pl.BlockSpec((tm, d), lambda i: (i, 0)),
        out_shape=jax.ShapeDtypeStruct((nrows, d), F32),
        compiler_params=_cparams(("parallel",)),
        name="final_norm",
    )(h, peer_t, g.reshape(1, d))


def _tiles(n, n_p, n_s):
    def pick(cands, m=n):
        for c in cands:
            if m % c == 0:
                return c
        raise ValueError(f"no tile for {m}")
    both = math.gcd(n_p, n_s)
    return dict(
        rms_tm=pick((256, 128, 64, 32, 16), both),
        mm_tm=pick((768, 512, 256, 128, 64, 32, 16)),
        proj_tm=pick((1408, 768, 512, 256, 128, 64, 32, 16)),
        mm_tm_p=pick((1024, 512, 256, 128, 64, 32, 16), n_p),
        mm_tm_s=pick((256, 128, 64, 32, 16), n_s),
        peer_tt=pick((768, 512, 256, 128)),
        route_tn=pick((256, 128)),
        final_tm=pick((256, 128), both),
    )


def _alibi_slopes(nheads):
    return [2.0 ** (-8.0 * (i + 1) / nheads) for i in range(nheads)]


def kernel(x_prompt, x_sample, cache_k, cache_v, state_gla, page_table, norm1_gain, w_in, w_a2, b_a, gla_gain,
           lambda_q1, lambda_k1, lambda_q2, lambda_k2, diff_gain, w_o, norm2_gain, peer_wq, peer_keys1,
           peer_keys2, peer_u, peer_v, final_gain):
    depth = w_in.shape[0]
    assert depth == 1
    B, T, D = x_prompt.shape
    DB, TS, _ = x_sample.shape
    n_p, n_s = B * T, DB * TS
    n = n_p + n_s
    gla_dv = D // (2 * GLA_HEADS)
    gla_dk = gla_dv // 2
    diff_dv = D // (2 * DIFF_HEADS)
    diff_dk = diff_dv // 2
    gqk_w = GLA_HEADS * gla_dk
    gv_w = GLA_HEADS * gla_dv
    dqk_w = DIFF_HEADS * 2 * diff_dk
    dv_w = DIFF_HEADS * diff_dv
    g_w = 2 * gqk_w + 2 * gv_w
    tl = _tiles(n, n_p, n_s)
    l = 0
    lam_init = 0.8 - 0.6 * math.exp(-0.3 * l)
    xp = x_prompt.reshape(n_p, D)
    xs = x_sample.reshape(n_s, D)

    wt = w_in[l].T
    wt_g = wt[:g_w].astype(BF16)
    wt_d = wt[g_w + GLA_LOWRANK:].astype(BF16)
    wt_ga = jnp.pad(wt[g_w:g_w + GLA_LOWRANK], ((0, LANES - GLA_LOWRANK), (0, 0))).astype(BF16)
    wa2p = jnp.pad(w_a2[l], ((0, LANES - GLA_LOWRANK), (0, 0)))
    w_ob = w_o[l].astype(BF16)
    f32 = F32
    lam = (jnp.exp(jnp.sum(lambda_q1[l].astype(f32) * lambda_k1[l].astype(f32)))
           - jnp.exp(jnp.sum(lambda_q2[l].astype(f32) * lambda_k2[l].astype(f32))) + lam_init).reshape(1)
    slopes = _alibi_slopes(DIFF_HEADS)

    xn = rms_norm(xp, xs, norm1_gain[l], tl["rms_tm"])
    proj_g = matmul_wt(xn, wt_g, tl["proj_tm"], 512, name="proj_gla")
    proj_d = matmul_wt(xn, wt_d, tl["proj_tm"], 512, name="proj_diff")
    ga = matmul_wt(xn, wt_ga, tl["mm_tm"], LANES, name="proj_gate")

    gla_args = (proj_g, ga, wa2p, b_a[l].reshape(1, -1), gla_gain[l].reshape(1, -1))
    og_p, s_p = gla(*gla_args, None, nbatch=B, seqlen=T, chunk=64, valid=64, row0=0, out_cols=gv_w,
                    out_dtype=BF16, dk=gla_dk, dv=gla_dv, name="gla_prompt")
    og_s, s_s = gla(*gla_args, state_gla[l], nbatch=DB, seqlen=TS, chunk=GLA_SUB, valid=TS, row0=n_p,
                    out_cols=gv_w, out_dtype=F32, dk=gla_dk, dv=gla_dv, name="gla_sample")

    out_scale = 1.0 - lam_init
    dgain = diff_gain[l].reshape(1, -1)
    od_p = attn_prompt(proj_d, jnp.asarray(slopes, F32), lam, dgain, nbatch=B, seqlen=T, tq=min(512, T),
                       col_q=0, col_k=dqk_w, col_v=2 * dqk_w, dk=diff_dk, dv=diff_dv, out_scale=out_scale)
    n_pool = cache_k.shape[1]
    ck = cache_k[l].reshape(n_pool, PAGE_SIZE * DIFF_HEADS * 2, diff_dk)
    od_s = attn_sample(proj_d, ck, cache_v[l], page_table, slopes, lam, dgain, nbatch=DB, tnew=TS, row0=n_p,
                       pg=4, colblk_q=0, colblk_k=1, colblk_v=2, dk=diff_dk, dv=diff_dv, out_scale=out_scale)

    h_p = matmul2_res(og_p, od_p, w_ob, xp, tl["mm_tm_p"], 1024, name="out_proj")
    h_s = matmul2_res(og_s.astype(BF16), od_s.astype(BF16), w_ob, xs, tl["mm_tm_s"], 1024,
                      name="out_proj_sample")

    xn2 = rms_norm(h_p, h_s, norm2_gain[l], tl["rms_tm"])
    qt = matmul_nt(peer_wq[l].T.astype(BF16), xn2, PEER_HEADS * PEER_DQ, tl["peer_tt"], name="peer_query")
    e1t, e2t, tht = peer_route(qt, peer_keys1[l], peer_keys2[l], tl["route_tn"])
    act_t = peer_act(xn2, peer_u[l].astype(BF16), e1t, e2t, tht, tt=tl["peer_tt"], te=1024)
    peer_t = matmul_kacc(peer_v[l].T.astype(BF16), act_t, D, tl["peer_tt"], 1024, name="peer_out")

    y_p = final_norm(h_p, peer_t, final_gain, col0=0, tm=tl["final_tm"])
    y_s = final_norm(h_s, peer_t, final_gain, col0=n_p, tm=tl["final_tm"])

    k_p = proj_d[:n_p, dqk_w:2 * dqk_w].reshape(1, B, T, DIFF_HEADS, 2, diff_dk)
    v_p = proj_d[:n_p, 2 * dqk_w:].reshape(1, B, T, DIFF_HEADS, diff_dv)
    k_s = proj_d[n_p:, dqk_w:2 * dqk_w].reshape(1, DB, TS, DIFF_HEADS, 2, diff_dk)
    v_s = proj_d[n_p:, 2 * dqk_w:].reshape(1, DB, TS, DIFF_HEADS, diff_dv)
    return (y_p.reshape(B, T, D), y_s.reshape(DB, TS, D), k_p, v_p, s_p[None], k_s, v_s, s_s[None])
```

```python
import functools
import math

import jax
import jax.numpy as jnp
import numpy as np
from jax import lax
from jax.experimental import pallas as pl
from jax.experimental.pallas import tpu as pltpu

F32 = jnp.float32
BF16 = jnp.bfloat16

GLA_HEADS = 4
GLA_LOWRANK = 16
GLA_GATE_TEMP = 16.0
DIFF_HEADS = 8
PEER_HEADS = 8
PEER_NKEYS = 128
PEER_DQ = 128
PEER_TOPK = 16
PAGE_SIZE = 128
RMS_EPS = 1e-6
LANES = 128
SUBLANES = 8
VMEM_LIMIT = 56 * 1024 * 1024
LOG2E = math.log2(math.e)

NT_DIMS = (((1,), (1,)), ((), ()))
TN_DIMS = (((0,), (0,)), ((), ()))
NEG_INF = float("-inf")


def _cparams(sem):
    return pltpu.CompilerParams(dimension_semantics=sem, vmem_limit_bytes=VMEM_LIMIT)


def _rms_kernel(xa_ref, xb_ref, g_ref, o_ref, *, na):
    def norm(x_ref):
        x = x_ref[...]
        ms = jnp.mean(x * x, axis=-1, keepdims=True)
        o_ref[...] = (x * lax.rsqrt(ms + RMS_EPS) * g_ref[...]).astype(o_ref.dtype)

    i = pl.program_id(0)

    @pl.when(i < na)
    def _():
        norm(xa_ref)

    @pl.when(i >= na)
    def _():
        norm(xb_ref)


def rms_norm(xa, xb, g, tm, out_dtype=BF16):
    d = xa.shape[1]
    na, nb = xa.shape[0] // tm, xb.shape[0] // tm
    return pl.pallas_call(
        functools.partial(_rms_kernel, na=na),
        grid=(na + nb,),
        in_specs=[pl.BlockSpec((tm, d), lambda i: (jnp.minimum(i, na - 1), 0)),
                  pl.BlockSpec((tm, d), lambda i: (jnp.maximum(i - na, 0), 0)),
                  pl.BlockSpec((1, d), lambda i: (0, 0))],
        out_specs=pl.BlockSpec((tm, d), lambda i: (i, 0)),
        out_shape=jax.ShapeDtypeStruct(((na + nb) * tm, d), out_dtype),
        compiler_params=_cparams(("arbitrary",)),
        name="rms_norm",
    )(xa, xb, g.reshape(1, d))


def _mm_wt_kernel(x_ref, wt_ref, o_ref):
    o_ref[...] = lax.dot_general(x_ref[...], wt_ref[...], NT_DIMS, preferred_element_type=F32)


def matmul_wt(x, wt, tm, tn, name="matmul_wt"):
    n, k = x.shape
    m = wt.shape[0]
    return pl.pallas_call(
        _mm_wt_kernel,
        grid=(n // tm, m // tn),
        in_specs=[pl.BlockSpec((tm, k), lambda i, j: (i, 0)),
                  pl.BlockSpec((tn, k), lambda i, j: (j, 0))],
        out_specs=pl.BlockSpec((tm, tn), lambda i, j: (i, j)),
        out_shape=jax.ShapeDtypeStruct((n, m), F32),
        compiler_params=_cparams(("parallel", "parallel")),
        name=name,
    )(x, wt)


def _mm2_res_kernel(x1_ref, x2_ref, w_ref, r_ref, o_ref):
    k1 = x1_ref.shape[1]
    o_ref[...] = (r_ref[...] + jnp.dot(x1_ref[...], w_ref[:k1, :], preferred_element_type=F32)
                  + jnp.dot(x2_ref[...], w_ref[k1:, :], preferred_element_type=F32))


def matmul2_res(x1, x2, w, res, tm, tn, name="matmul2_res"):
    n, k1 = x1.shape
    k2 = x2.shape[1]
    m = w.shape[1]
    return pl.pallas_call(
        _mm2_res_kernel,
        grid=(n // tm, m // tn),
        in_specs=[pl.BlockSpec((tm, k1), lambda i, j: (i, 0)),
                  pl.BlockSpec((tm, k2), lambda i, j: (i, 0)),
                  pl.BlockSpec((k1 + k2, tn), lambda i, j: (0, j)),
                  pl.BlockSpec((tm, tn), lambda i, j: (i, j))],
        out_specs=pl.BlockSpec((tm, tn), lambda i, j: (i, j)),
        out_shape=jax.ShapeDtypeStruct((n, m), F32),
        compiler_params=_cparams(("parallel", "parallel")),
        name=name,
    )(x1, x2, w, res)


def _mm_nt_kernel(a_ref, b_ref, o_ref):
    o_ref[...] = lax.dot_general(a_ref[...], b_ref[...], NT_DIMS, preferred_element_type=F32)


def matmul_nt(a, b, tm, tn, name="matmul_nt"):
    m, k = a.shape
    n = b.shape[0]
    return pl.pallas_call(
        _mm_nt_kernel,
        grid=(n // tn, m // tm),
        in_specs=[pl.BlockSpec((tm, k), lambda j, i: (i, 0)),
                  pl.BlockSpec((tn, k), lambda j, i: (j, 0))],
        out_specs=pl.BlockSpec((tm, tn), lambda j, i: (i, j)),
        out_shape=jax.ShapeDtypeStruct((m, n), F32),
        compiler_params=_cparams(("parallel", "parallel")),
        name=name,
    )(a, b)


def _mm_acc_kernel(a_ref, b_ref, o_ref):
    @pl.when(pl.program_id(2) == 0)
    def _():
        o_ref[...] = jnp.zeros_like(o_ref)

    tm = o_ref.shape[0]
    rc = min(tm, 1024)
    for r in range(tm // rc):
        o_ref[r * rc:(r + 1) * rc, :] += jnp.dot(a_ref[r * rc:(r + 1) * rc, :], b_ref[...],
                                                 preferred_element_type=F32)


def matmul_kacc(a, b, tm, tn, tk, name="matmul_kacc"):
    m, k = a.shape
    n = b.shape[1]
    return pl.pallas_call(
        _mm_acc_kernel,
        grid=(m // tm, n // tn, k // tk),
        in_specs=[pl.BlockSpec((tm, tk), lambda i, j, kk: (i, kk)),
                  pl.BlockSpec((tk, tn), lambda i, j, kk: (kk, j))],
        out_specs=pl.BlockSpec((tm, tn), lambda i, j, kk: (i, j)),
        out_shape=jax.ShapeDtypeStruct((m, n), F32),
        compiler_params=_cparams(("parallel", "parallel", "arbitrary")),
        name=name,
    )(a, b)


GLA_SUB = 16


def _log_sigmoid(x):
    return jnp.minimum(x, 0.0) - jnp.log1p(jnp.exp(-jnp.abs(x)))


def _gla_chunk(q, k, v, r, ga, wa, ba, gain, sts, *, chunk, valid, dk, dv):
    nh = GLA_HEADS
    hk = [slice(h * dk, (h + 1) * dk) for h in range(nh)]
    hv = [slice(h * dv, (h + 1) * dv) for h in range(nh)]
    q = q * (dk ** -0.5)
    pre = jnp.dot(ga, wa, preferred_element_type=F32, precision=lax.Precision.HIGHEST) + ba
    g = _log_sigmoid(pre) * (1.0 / GLA_GATE_TEMP)
    if valid < chunk:
        g = jnp.where(lax.broadcasted_iota(jnp.int32, (chunk, 1), 0) < valid, g, 0.0)
    ri = lax.broadcasted_iota(jnp.int32, (chunk, chunk), 0)
    ci = lax.broadcasted_iota(jnp.int32, (chunk, chunk), 1)
    tril = (ci <= ri).astype(F32)
    G = jnp.dot(tril, g, preferred_element_type=F32, precision=lax.Precision.HIGHEST)
    g_last = G[chunk - 1:chunk, :]

    qg = (q * jnp.exp(G)).astype(BF16)
    o = [lax.dot_general(qg[:, hk[h]], sts[h].astype(BF16), NT_DIMS, preferred_element_type=F32)
         for h in range(nh)]

    sub = min(GLA_SUB, chunk)
    lane = lax.broadcasted_iota(jnp.int32, (sub, chunk), 1)
    srow = lax.broadcasted_iota(jnp.int32, (sub, chunk), 0)
    a_rows = [[] for _ in range(nh)]
    for i in range(chunk // sub):
        lo = i * sub
        Gi = G[lo:lo + sub, :]
        qi = q[lo:lo + sub, :]
        if i > 0:
            ref_i = G[lo - 1:lo, :]
            qs = (qi * jnp.exp(Gi - ref_i)).astype(BF16)
            ks = (k * jnp.exp(jnp.minimum(ref_i - G, 0.0))).astype(BF16)
            a_i = [jnp.where(lane < lo, lax.dot_general(qs[:, hk[h]], ks[:, hk[h]], NT_DIMS,
                                                        preferred_element_type=F32), 0.0) for h in range(nh)]
        else:
            a_i = [jnp.zeros((sub, chunk), F32) for _ in range(nh)]
        for s in range(sub):
            w = qi * jnp.exp(jnp.minimum(Gi - G[lo + s:lo + s + 1, :], 0.0)) * k[lo + s:lo + s + 1, :]
            for h in range(nh):
                col = jnp.sum(w[:, hk[h]], axis=1, keepdims=True)
                a_i[h] = jnp.where(lane == lo + s, col, a_i[h])
        for h in range(nh):
            a_rows[h].append(jnp.where(lane <= lo + srow, a_i[h], 0.0))

    kd = (k * jnp.exp(g_last - G)).astype(BF16)
    decay = jnp.exp(g_last)
    vb = v.astype(BF16)
    rg = r * jax.nn.sigmoid(r)
    outs, sts_new = [], []
    for h in range(nh):
        a_mat = a_rows[h][0] if len(a_rows[h]) == 1 else jnp.concatenate(a_rows[h], axis=0)
        oh = o[h] + jnp.dot(a_mat.astype(BF16), vb[:, hv[h]], preferred_element_type=F32)
        upd = lax.dot_general(vb[:, hv[h]], kd[:, hk[h]], TN_DIMS, preferred_element_type=F32)
        sts_new.append(sts[h] * decay[:, hk[h]] + upd)
        oh = oh[:valid, :]
        ms = jnp.mean(oh * oh, axis=-1, keepdims=True)
        outs.append(oh * lax.rsqrt(ms + RMS_EPS) * gain[:, hv[h]] * rg[:, hv[h]])
    return outs, sts_new


def _gla_kernel(*refs, chunk, valid, has_s0, dk, dv):
    if has_s0:
        (q_ref, k_ref, v_ref, r_ref, ga_ref, wa_ref, ba_ref, gain_ref, s0_ref, o_ref, s_ref, st_scr) = refs
    else:
        (q_ref, k_ref, v_ref, r_ref, ga_ref, wa_ref, ba_ref, gain_ref, o_ref, s_ref, st_scr) = refs
        s0_ref = None
    c = pl.program_id(1)
    nc = pl.num_programs(1)

    @pl.when(c == 0)
    def _():
        for h in range(GLA_HEADS):
            if has_s0:
                st_scr[h] = s0_ref[h].T
            else:
                st_scr[h] = jnp.zeros((dv, dk), F32)

    def rows(x):
        if valid < chunk:
            x = jnp.concatenate([x, jnp.zeros((chunk - valid, x.shape[1]), x.dtype)], axis=0)
        return x

    outs, sts_new = _gla_chunk(rows(q_ref[...]), rows(k_ref[...]), rows(v_ref[...]), r_ref[...],
                               rows(ga_ref[...]), wa_ref[...], ba_ref[...], gain_ref[...],
                               [st_scr[h] for h in range(GLA_HEADS)], chunk=chunk, valid=valid, dk=dk, dv=dv)
    for h in range(GLA_HEADS):
        st_scr[h] = sts_new[h]
        o_ref[:, h * dv:(h + 1) * dv] = outs[h].astype(o_ref.dtype)

    @pl.when(c == nc - 1)
    def _():
        for h in range(GLA_HEADS):
            s_ref[h] = sts_new[h].T


def gla(p, ga, wa2p, b_a, gain, s0, *, nbatch, seqlen, chunk, valid, row0, out_cols, out_dtype, dk, dv, name):
    nc = seqlen // valid
    rb0 = row0 // valid
    h_n = GLA_HEADS
    qk_w, v_w = h_n * dk, h_n * dv
    assert v_w == 2 * qk_w
    has_s0 = s0 is not None

    def rowblk(b, c):
        return rb0 + b * nc + c

    in_specs = [
        pl.BlockSpec((valid, qk_w), lambda b, c: (rowblk(b, c), 0)),
        pl.BlockSpec((valid, qk_w), lambda b, c: (rowblk(b, c), 1)),
        pl.BlockSpec((valid, v_w), lambda b, c: (rowblk(b, c), 1)),
        pl.BlockSpec((valid, v_w), lambda b, c: (rowblk(b, c), 2)),
        pl.BlockSpec((valid, LANES), lambda b, c: (rowblk(b, c), 0)),
        pl.BlockSpec((LANES, qk_w), lambda b, c: (0, 0)),
        pl.BlockSpec((1, qk_w), lambda b, c: (0, 0)),
        pl.BlockSpec((1, v_w), lambda b, c: (0, 0)),
    ]
    args = [p, p, p, p, ga, wa2p, b_a, gain]
    if has_s0:
        in_specs.append(pl.BlockSpec((None, h_n, dk, dv), lambda b, c: (b, 0, 0, 0)))
        args.append(s0)
    kern = functools.partial(_gla_kernel, chunk=chunk, valid=valid, has_s0=has_s0, dk=dk, dv=dv)
    return pl.pallas_call(
        kern,
        grid=(nbatch, nc),
        in_specs=in_specs,
        out_specs=[pl.BlockSpec((valid, v_w), lambda b, c: (b * nc + c, 0)),
                   pl.BlockSpec((None, h_n, dk, dv), lambda b, c: (b, 0, 0, 0))],
        out_shape=[jax.ShapeDtypeStruct((nbatch * seqlen, out_cols), out_dtype),
                   jax.ShapeDtypeStruct((nbatch, h_n, dk, dv), F32)],
        scratch_shapes=[pltpu.VMEM((h_n, dv, dk), F32)],
        compiler_params=_cparams(("parallel", "arbitrary")),
        name=name,
    )(*args)


def _attn_prompt_kernel(qi_ref, ki_ref, slopes_ref, lam_ref, q_ref, k_ref, v_ref, gain_ref, o_ref,
                        m_scr, l_scr, acc_scr, *, tq, tk, dk, dv, out_scale):
    h = pl.program_id(1)
    step_id = pl.program_id(2)
    qi = qi_ref[step_id]
    ki = ki_ref[step_id]

    @pl.when(ki == 0)
    def _():
        m_scr[...] = jnp.full_like(m_scr, NEG_INF)
        l_scr[...] = jnp.zeros_like(l_scr)
        acc_scr[...] = jnp.zeros_like(acc_scr)

    def step(masked):
        krel = (ki * tk - qi * tq + lax.broadcasted_iota(jnp.int32, (1, tk), 1)).astype(F32)
        kbias = (slopes_ref[h] * LOG2E) * krel
        vb = v_ref[...].astype(BF16)
        if masked:
            keep = (lax.broadcasted_iota(jnp.int32, (tq, tk), 0)
                    >= lax.broadcasted_iota(jnp.int32, (tq, tk), 1))
        streams = range(2)
        qb = (q_ref[...] * (dk ** -0.5 * LOG2E)).astype(BF16)
        kb = k_ref[...].astype(BF16)
        s = [lax.dot_general(qb[:, c * dk:(c + 1) * dk], kb[:, c * dk:(c + 1) * dk], NT_DIMS,
                             preferred_element_type=F32) + kbias for c in streams]
        if masked:
            s = [jnp.where(keep, sc, NEG_INF) for sc in s]
        m_prev = [m_scr[c] for c in streams]
        m_new = [jnp.maximum(m_prev[c], jnp.max(s[c], axis=-1, keepdims=True)) for c in streams]
        alpha = [jnp.exp2(m_prev[c] - m_new[c]) for c in streams]
        p = [jnp.exp2(s[c] - jnp.tile(m_new[c], (1, tk // LANES))) for c in streams]
        psum = [jnp.sum(p[c], axis=-1, keepdims=True) for c in streams]
        pv = [jnp.dot(p[c].astype(BF16), vb, preferred_element_type=F32) for c in streams]
        for c in streams:
            l_scr[c] = alpha[c] * l_scr[c] + psum[c]
            acc_scr[c] = jnp.tile(alpha[c], (1, dv // LANES)) * acc_scr[c] + pv[c]
            m_scr[c] = m_new[c]

    @pl.when(ki < qi)
    def _():
        step(False)

    @pl.when(ki == qi)
    def _():
        step(True)
        lam = lam_ref[0]
        reps = (1, dv // LANES)
        o = acc_scr[0] / jnp.tile(l_scr[0], reps) - lam * (acc_scr[1] / jnp.tile(l_scr[1], reps))
        ms = jnp.mean(o * o, axis=-1, keepdims=True)
        o = o * lax.rsqrt(ms + RMS_EPS) * gain_ref[...] * out_scale
        o_ref[...] = o.astype(o_ref.dtype)


def attn_prompt(p, slopes, lam, gain, *, nbatch, seqlen, tq, col_q, col_k, col_v, dk, dv, out_scale):
    nq = seqlen // tq
    pairs = [(i, j) for i in range(nq) for j in range(i + 1)]
    qi_tab = jnp.asarray([i for i, _ in pairs], jnp.int32)
    ki_tab = jnp.asarray([j for _, j in pairs], jnp.int32)
    kern = functools.partial(_attn_prompt_kernel, tq=tq, tk=tq, dk=dk, dv=dv, out_scale=out_scale)
    smem = pl.BlockSpec(memory_space=pltpu.SMEM)
    grid_spec = pltpu.PrefetchScalarGridSpec(
        num_scalar_prefetch=2,
        grid=(nbatch, DIFF_HEADS, len(pairs)),
        in_specs=[smem, smem,
                  pl.BlockSpec((tq, 2 * dk), lambda b, h, s, qt, kt: (b * nq + qt[s], col_q // (2 * dk) + h)),
                  pl.BlockSpec((tq, 2 * dk), lambda b, h, s, qt, kt: (b * nq + kt[s], col_k // (2 * dk) + h)),
                  pl.BlockSpec((tq, dv), lambda b, h, s, qt, kt: (b * nq + kt[s], col_v // dv + h)),
                  pl.BlockSpec((1, dv), lambda b, h, s, qt, kt: (0, 0))],
        out_specs=pl.BlockSpec((tq, dv), lambda b, h, s, qt, kt: (b * nq + qt[s], h)),
        scratch_shapes=[pltpu.VMEM((2, tq, LANES), F32), pltpu.VMEM((2, tq, LANES), F32),
                        pltpu.VMEM((2, tq, dv), F32)],
    )
    return pl.pallas_call(
        kern,
        grid_spec=grid_spec,
        out_shape=jax.ShapeDtypeStruct((nbatch * seqlen, DIFF_HEADS * dv), BF16),
        compiler_params=_cparams(("parallel", "parallel", "arbitrary")),
        name="attn_prompt",
    )(qi_tab, ki_tab, slopes, lam, p, p, p, gain)


def _attn_sample_kernel(pt_ref, lam_ref, q_ref, kn_ref, vn_ref, gain_ref, c0n_ref, c0s_ref, scol_ref, *rest,
                        pg, tnew, past_len, dk, dv, out_scale):
    k_refs = rest[:pg]
    v_refs = rest[pg:2 * pg]
    o_ref = rest[2 * pg]
    qt_scr, m_scr, l_scr, acc_scr = rest[2 * pg + 1:]
    j = pl.program_id(1)
    nj = pl.num_programs(1)
    nh = DIFF_HEADS
    half = nh * tnew
    page_lanes = PAGE_SIZE * nh

    @pl.when(j == 0)
    def _():
        s_parts = []
        for c in range(2):
            cols = [slice((2 * h + c) * dk, (2 * h + c + 1) * dk) for h in range(nh)]
            qt = jnp.concatenate([q_ref[:, cs] for cs in cols], axis=0) * (dk ** -0.5 * LOG2E)
            qt_scr[c] = qt.astype(BF16)
            kn = jnp.concatenate([kn_ref[:, cs] for cs in cols], axis=0)
            s_parts.append(lax.dot_general(qt, kn, NT_DIMS, preferred_element_type=F32))
        s = jnp.concatenate(s_parts, axis=0) - c0n_ref[...]
        m = jnp.max(s, axis=-1, keepdims=True)
        p = jnp.exp2(s - m)
        vn = jnp.concatenate([vn_ref[:, h * dv:(h + 1) * dv] for h in range(nh)], axis=0)
        m_scr[...] = m
        l_scr[...] = jnp.sum(p, axis=-1, keepdims=True)
        acc_scr[...] = jnp.dot(p, vn, preferred_element_type=F32)

    s_parts = []
    for c in range(2):
        qt = qt_scr[c]
        s_parts.append(jnp.concatenate(
            [lax.dot_general(qt, k_refs[i][pl.ds(c, page_lanes, stride=2), :].astype(BF16), NT_DIMS,
                             preferred_element_type=F32) for i in range(pg)], axis=-1))
    s = jnp.concatenate(s_parts, axis=0) - c0s_ref[...]
    addcol = scol_ref[...] * (j * (pg * PAGE_SIZE) - past_len).astype(F32)
    m_prev = m_scr[...]
    m_new = jnp.maximum(m_prev, jnp.max(s, axis=-1, keepdims=True) + addcol)
    alpha = jnp.exp2(m_prev - m_new)
    p = jnp.exp2(s + (addcol - m_new))
    l_scr[...] = alpha * l_scr[...] + jnp.sum(p, axis=-1, keepdims=True)
    pb = p.astype(BF16)
    acc = alpha * acc_scr[...]
    for i in range(pg):
        vi = v_refs[i][...].reshape(page_lanes, dv).astype(BF16)
        acc = acc + jnp.dot(pb[:, i * page_lanes:(i + 1) * page_lanes], vi, preferred_element_type=F32)
    acc_scr[...] = acc
    m_scr[...] = m_new

    @pl.when(j == nj - 1)
    def _():
        lam = lam_ref[0]
        on = acc / l_scr[...]
        for h in range(nh):
            o = on[h * tnew:(h + 1) * tnew, :] - lam * on[half + h * tnew:half + (h + 1) * tnew, :]
            ms = jnp.mean(o * o, axis=-1, keepdims=True)
            o_ref[:, h * dv:(h + 1) * dv] = o * lax.rsqrt(ms + RMS_EPS) * gain_ref[...] * out_scale


def _sample_bias_tiles(slopes, tnew, pg):
    nh = DIFF_HEADS
    r = np.arange(2 * nh * tnew)
    rh, rq = (r // tnew) % nh, r % tnew
    sl = np.asarray(slopes, np.float64)[rh] * LOG2E
    ln = np.arange(nh * tnew)
    lh, lt = ln // tnew, ln % tnew
    ok = (lh[None, :] == rh[:, None]) & (lt[None, :] <= rq[:, None])
    c0n = np.where(ok, sl[:, None] * (rq[:, None] - lt[None, :]), np.inf)
    ls = np.arange(pg * PAGE_SIZE * nh)
    lh, lt = ls % nh, ls // nh
    ok = lh[None, :] == rh[:, None]
    c0s = np.where(ok, sl[:, None] * (rq[:, None] - lt[None, :]), np.inf)
    return (jnp.asarray(c0n, F32), jnp.asarray(c0s, F32), jnp.asarray(sl[:, None], F32))


def attn_sample(p, cache_k, cache_v, page_table, slopes, lam, gain, *, nbatch, tnew, row0, pg,
                colblk_q, colblk_k, colblk_v, dk, dv, out_scale):
    n_pages = page_table.shape[1]
    nj = n_pages // pg
    nh = DIFF_HEADS
    width = nh * dv
    rb0 = row0 // tnew
    rows = 2 * nh * tnew
    c0n, c0s, scol = _sample_bias_tiles(slopes, tnew, pg)
    kern = functools.partial(_attn_sample_kernel, pg=pg, tnew=tnew, past_len=n_pages * PAGE_SIZE,
                             dk=dk, dv=dv, out_scale=out_scale)
    smem = pl.BlockSpec(memory_space=pltpu.SMEM)

    def kpage_spec(i):
        return pl.BlockSpec((None, PAGE_SIZE * nh * 2, dk), lambda b, j, pt: (pt[b, j * pg + i], 0, 0))

    def vpage_spec(i):
        return pl.BlockSpec((None, PAGE_SIZE, nh, dv), lambda b, j, pt: (pt[b, j * pg + i], 0, 0, 0))

    in_specs = [smem,
                pl.BlockSpec((tnew, width), lambda b, j, pt: (rb0 + b, colblk_q)),
                pl.BlockSpec((tnew, width), lambda b, j, pt: (rb0 + b, colblk_k)),
                pl.BlockSpec((tnew, width), lambda b, j, pt: (rb0 + b, colblk_v)),
                pl.BlockSpec((1, dv), lambda b, j, pt: (0, 0)),
                pl.BlockSpec(c0n.shape, lambda b, j, pt: (0, 0)),
                pl.BlockSpec(c0s.shape, lambda b, j, pt: (0, 0), pipeline_mode=pl.Buffered(1)),
                pl.BlockSpec((rows, 1), lambda b, j, pt: (0, 0))]
    in_specs += [kpage_spec(i) for i in range(pg)] + [vpage_spec(i) for i in range(pg)]
    grid_spec = pltpu.PrefetchScalarGridSpec(
        num_scalar_prefetch=1,
        grid=(nbatch, nj),
        in_specs=in_specs,
        out_specs=pl.BlockSpec((tnew, width), lambda b, j, pt: (b, 0)),
        scratch_shapes=[pltpu.VMEM((2, nh * tnew, dk), BF16),
                        pltpu.VMEM((rows, 1), F32),
                        pltpu.VMEM((rows, 1), F32),
                        pltpu.VMEM((rows, dv), F32)],
    )
    return pl.pallas_call(
        kern,
        grid_spec=grid_spec,
        out_shape=jax.ShapeDtypeStruct((nbatch * tnew, width), F32),
        compiler_params=_cparams(("parallel", "arbitrary")),
        name="attn_sample",
    )(page_table, lam, p, p, p, gain, c0n, c0s, scol, *([cache_k] * pg), *([cache_v] * pg))


N_TOP = PEER_TOPK + 1


def _top_values(x, n):
    cur = jnp.max(x, axis=0, keepdims=True)
    vals = [cur]
    for _ in range(n - 1):
        cur = jnp.max(jnp.where(x < cur, x, NEG_INF), axis=0, keepdims=True)
        vals.append(cur)
    return vals


_CAND_PAIRS = [(a, b) for a in range(N_TOP) for b in range(N_TOP) if (a + 1) * (b + 1) <= N_TOP]
_CAND_ROWS = -(-len(_CAND_PAIRS) // SUBLANES) * SUBLANES


def _route_kernel(qt_ref, k1_ref, k2_ref, e1_ref, e2_ref, th_ref, cand_scr):
    half = PEER_DQ // 2
    tn = qt_ref.shape[1]
    cand_scr[...] = jnp.full_like(cand_scr, NEG_INF)
    for h in range(PEER_HEADS):
        q1 = qt_ref[h * PEER_DQ:h * PEER_DQ + half, :]
        q2 = qt_ref[h * PEER_DQ + half:(h + 1) * PEER_DQ, :]
        s1 = jnp.dot(k1_ref[...], q1, preferred_element_type=F32, precision=lax.Precision.HIGHEST)
        s2 = jnp.dot(k2_ref[...], q2, preferred_element_type=F32, precision=lax.Precision.HIGHEST)
        t1 = _top_values(s1, N_TOP)
        t2 = _top_values(s2, N_TOP)
        for i, (a, b) in enumerate(_CAND_PAIRS):
            cand_scr[i:i + 1, :] = t1[a] + t2[b]
        c = _top_values(cand_scr[...], N_TOP)
        m = c[0]
        z = jnp.zeros_like(m)
        for i in range(PEER_TOPK):
            z = z + jnp.exp(c[i] - m)
        scale = 0.5 / z
        e1 = jnp.exp(s1 - t1[0])
        e2 = jnp.exp(s2 - t2[0]) * scale
        thr = jnp.exp(0.5 * (c[PEER_TOPK - 1] + c[PEER_TOPK]) - m) * scale
        for cb in range(tn // LANES):
            e1_ref[h, cb] = e1[:, cb * LANES:(cb + 1) * LANES]
            e2_ref[h, cb] = e2[:, cb * LANES:(cb + 1) * LANES]
            th_ref[cb, h:h + 1, :] = thr[:, cb * LANES:(cb + 1) * LANES]


def peer_route(qt, keys1, keys2, tn):
    n = qt.shape[1]
    nk = PEER_NKEYS
    nb = tn // LANES
    tile4 = pl.BlockSpec((PEER_HEADS, nb, nk, LANES), lambda i: (0, i, 0, 0))
    shape4 = jax.ShapeDtypeStruct((PEER_HEADS, n // LANES, nk, LANES), F32)
    return pl.pallas_call(
        _route_kernel,
        grid=(n // tn,),
        in_specs=[pl.BlockSpec((PEER_HEADS * PEER_DQ, tn), lambda i: (0, i)),
                  pl.BlockSpec((nk, PEER_DQ // 2), lambda i: (0, 0)),
                  pl.BlockSpec((nk, PEER_DQ // 2), lambda i: (0, 0))],
        out_specs=[tile4, tile4, pl.BlockSpec((nb, PEER_HEADS, LANES), lambda i: (i, 0, 0))],
        out_shape=[shape4, shape4, jax.ShapeDtypeStruct((n // LANES, PEER_HEADS, LANES), F32)],
        scratch_shapes=[pltpu.VMEM((_CAND_ROWS, tn), F32)],
        compiler_params=_cparams(("parallel",)),
        name="peer_route",
    )(qt, keys1, keys2)


_GELU_C = math.sqrt(2.0 / math.pi)


def _peer_act_kernel(x_ref, u_ref, e2_ref, e1_ref, th_ref, o_ref):
    te, tt = o_ref.shape
    nk = PEER_NKEYS
    pre = lax.dot_general(u_ref[...], x_ref[...], NT_DIMS, preferred_element_type=F32)
    for r in range(te // nk):
        for cb in range(tt // LANES):
            w = jnp.zeros((nk, LANES), F32)
            for h in range(PEER_HEADS):
                e1 = jnp.tile(jnp.broadcast_to(e1_ref[h, cb, r:r + 1, :], (SUBLANES, LANES)), (nk // SUBLANES, 1))
                th = jnp.tile(jnp.broadcast_to(th_ref[cb, h:h + 1, :], (SUBLANES, LANES)), (nk // SUBLANES, 1))
                p = e2_ref[h, cb] * e1
                w = w + jnp.where(p > th, p, 0.0)
            x = pre[r * nk:(r + 1) * nk, cb * LANES:(cb + 1) * LANES]
            g = x * (1.0 + jnp.tanh(_GELU_C * (x + 0.044715 * (x * x * x))))
            o_ref[r * nk:(r + 1) * nk, cb * LANES:(cb + 1) * LANES] = (g * w).astype(o_ref.dtype)


def peer_act(xn, u, e1t, e2t, tht, *, tt, te):
    n, d = xn.shape
    ne = u.shape[0]
    rows = te // PEER_NKEYS
    nb = tt // LANES
    return pl.pallas_call(
        _peer_act_kernel,
        grid=(n // tt, ne // te),
        in_specs=[pl.BlockSpec((tt, d), lambda t, e: (t, 0)),
                  pl.BlockSpec((te, d), lambda t, e: (e, 0)),
                  pl.BlockSpec((PEER_HEADS, nb, PEER_NKEYS, LANES), lambda t, e: (0, t, 0, 0)),
                  pl.BlockSpec((PEER_HEADS, nb, rows, LANES), lambda t, e: (0, t, e, 0)),
                  pl.BlockSpec((nb, PEER_HEADS, LANES), lambda t, e: (t, 0, 0))],
        out_specs=pl.BlockSpec((te, tt), lambda t, e: (e, t)),
        out_shape=jax.ShapeDtypeStruct((ne, n), BF16),
        compiler_params=_cparams(("parallel", "arbitrary")),
        name="peer_act",
    )(xn, u, e2t, e1t, tht)


def _final_kernel(h_ref, pt_ref, g_ref, o_ref):
    x = h_ref[...] + pt_ref[...].T
    ms = jnp.mean(x * x, axis=-1, keepdims=True)
    o_ref[...] = x * lax.rsqrt(ms + RMS_EPS) * g_ref[...]


def final_norm(h, peer_t, g, *, col0, tm):
    nrows, d = h.shape
    cb0 = col0 // tm
    return pl.pallas_call(
        _final_kernel,
        grid=(nrows // tm,),
        in_specs=[pl.BlockSpec((tm, d), lambda i: (i, 0)),
                  pl.BlockSpec((d, tm), lambda i: (0, cb0 + i)),
                  pl.BlockSpec((1, d), lambda i: (0, 0))],
        out_specs=pl.BlockSpec((tm, d), lambda i: (i, 0)),
        out_shape=jax.ShapeDtypeStruct((nrows, d), F32),
        compiler_params=_cparams(("parallel",)),
        name="final_norm",
    )(h, peer_t, g.reshape(1, d))


def _tiles(n, n_p, n_s):
    def pick(cands, m=n):
        for c in cands:
            if m % c == 0:
                return c
        raise ValueError(f"no tile for {m}")
    both = math.gcd(n_p, n_s)
    return dict(
        rms_tm=pick((256, 128, 64, 32, 16), both),
        mm_tm=pick((768, 512, 256, 128, 64, 32, 16)),
        proj_tm=pick((1408, 768, 512, 256, 128, 64, 32, 16)),
        mm_tm_p=pick((1024, 512, 256, 128, 64, 32, 16), n_p),
        mm_tm_s=pick((256, 128, 64, 32, 16), n_s),
        peer_tt=pick((768, 512, 256, 128)),
        route_tn=pick((256, 128)),
        final_tm=pick((256, 128), both),
    )


def _alibi_slopes(nheads):
    return [2.0 ** (-8.0 * (i + 1) / nheads) for i in range(nheads)]


def kernel(x_prompt, x_sample, cache_k, cache_v, state_gla, page_table, norm1_gain, w_in, w_a2, b_a, gla_gain,
           lambda_q1, lambda_k1, lambda_q2, lambda_k2, diff_gain, w_o, norm2_gain, peer_wq, peer_keys1,
           peer_keys2, peer_u, peer_v, final_gain):
    depth = w_in.shape[0]
    assert depth == 1
    B, T, D = x_prompt.shape
    DB, TS, _ = x_sample.shape
    n_p, n_s = B * T, DB * TS
    n = n_p + n_s
    gla_dv = D // (2 * GLA_HEADS)
    gla_dk = gla_dv // 2
    diff_dv = D // (2 * DIFF_HEADS)
    diff_dk = diff_dv // 2
    gqk_w = GLA_HEADS * gla_dk
    gv_w = GLA_HEADS * gla_dv
    dqk_w = DIFF_HEADS * 2 * diff_dk
    dv_w = DIFF_HEADS * diff_dv
    g_w = 2 * gqk_w + 2 * gv_w
    tl = _tiles(n, n_p, n_s)
    l = 0
    lam_init = 0.8 - 0.6 * math.exp(-0.3 * l)
    xp = x_prompt.reshape(n_p, D)
    xs = x_sample.reshape(n_s, D)

    wt = w_in[l].T
    wt_g = wt[:g_w].astype(BF16)
    wt_d = wt[g_w + GLA_LOWRANK:].astype(BF16)
    wt_ga = jnp.pad(wt[g_w:g_w + GLA_LOWRANK], ((0, LANES - GLA_LOWRANK), (0, 0))).astype(BF16)
    wa2p = jnp.pad(w_a2[l], ((0, LANES - GLA_LOWRANK), (0, 0)))
    w_ob = w_o[l].astype(BF16)
    f32 = F32
    lam = (jnp.exp(jnp.sum(lambda_q1[l].astype(f32) * lambda_k1[l].astype(f32)))
           - jnp.exp(jnp.sum(lambda_q2[l].astype(f32) * lambda_k2[l].astype(f32))) + lam_init).reshape(1)
    slopes = _alibi_slopes(DIFF_HEADS)

    xn = rms_norm(xp, xs, norm1_gain[l], tl["rms_tm"])
    proj_g = matmul_wt(xn, wt_g, tl["proj_tm"], 512, name="proj_gla")
    proj_d = matmul_wt(xn, wt_d, tl["proj_tm"], 512, name="proj_diff")
    ga = matmul_wt(xn, wt_ga, tl["mm_tm"], LANES, name="proj_gate")

    gla_args = (proj_g, ga, wa2p, b_a[l].reshape(1, -1), gla_gain[l].reshape(1, -1))
    og_p, s_p = gla(*gla_args, None, nbatch=B, seqlen=T, chunk=64, valid=64, row0=0, out_cols=gv_w,
                    out_dtype=BF16, dk=gla_dk, dv=gla_dv, name="gla_prompt")
    og_s, s_s = gla(*gla_args, state_gla[l], nbatch=DB, seqlen=TS, chunk=GLA_SUB, valid=TS, row0=n_p,
                    out_cols=gv_w, out_dtype=F32, dk=gla_dk, dv=gla_dv, name="gla_sample")

    out_scale = 1.0 - lam_init
    dgain = diff_gain[l].reshape(1, -1)
    od_p = attn_prompt(proj_d, jnp.asarray(slopes, F32), lam, dgain, nbatch=B, seqlen=T, tq=min(512, T),
                       col_q=0, col_k=dqk_w, col_v=2 * dqk_w, dk=diff_dk, dv=diff_dv, out_scale=out_scale)
    n_pool = cache_k.shape[1]
    ck = cache_k[l].reshape(n_pool, PAGE_SIZE * DIFF_HEADS * 2, diff_dk)
    od_s = attn_sample(proj_d, ck, cache_v[l], page_table, slopes, lam, dgain, nbatch=DB, tnew=TS, row0=n_p,
                       pg=min(8, page_table.shape[1]), colblk_q=0, colblk_k=1, colblk_v=2, dk=diff_dk,
                       dv=diff_dv, out_scale=out_scale)

    h_p = matmul2_res(og_p, od_p, w_ob, xp, tl["mm_tm_p"], 1024, name="out_proj")
    h_s = matmul2_res(og_s.astype(BF16), od_s.astype(BF16), w_ob, xs, tl["mm_tm_s"], 1024,
                      name="out_proj_sample")

    xn2 = rms_norm(h_p, h_s, norm2_gain[l], tl["rms_tm"])
    qt = matmul_nt(peer_wq[l].T.astype(BF16), xn2, PEER_HEADS * PEER_DQ, tl["peer_tt"], name="peer_query")
    e1t, e2t, tht = peer_route(qt, peer_keys1[l], peer_keys2[l], tl["route_tn"])
    act_t = peer_act(xn2, peer_u[l].astype(BF16), e1t, e2t, tht, tt=tl["peer_tt"], te=1024)
    peer_t = matmul_kacc(peer_v[l].T.astype(BF16), act_t, D, tl["peer_tt"], 1024, name="peer_out")

    y_p = final_norm(h_p, peer_t, final_gain, col0=0, tm=tl["final_tm"])
    y_s = final_norm(h_s, peer_t, final_gain, col0=n_p, tm=tl["final_tm"])

    k_p = proj_d[:n_p, dqk_w:2 * dqk_w].reshape(1, B, T, DIFF_HEADS, 2, diff_dk)
    v_p = proj_d[:n_p, 2 * dqk_w:].reshape(1, B, T, DIFF_HEADS, diff_dv)
    k_s = proj_d[n_p:, dqk_w:2 * dqk_w].reshape(1, DB, TS, DIFF_HEADS, 2, diff_dk)
    v_s = proj_d[n_p:, 2 * dqk_w:].reshape(1, DB, TS, DIFF_HEADS, diff_dv)
    return (y_p.reshape(B, T, D), y_s.reshape(DB, TS, D), k_p, v_p, s_p[None], k_s, v_s, s_s[None])
```

```python
import functools
import math

import jax
import jax.numpy as jnp
import numpy as np
from jax import lax
from jax.experimental import pallas as pl
from jax.experimental.pallas import tpu as pltpu

F32 = jnp.float32
BF16 = jnp.bfloat16

GLA_HEADS = 4
GLA_LOWRANK = 16
GLA_GATE_TEMP = 16.0
DIFF_HEADS = 8
PEER_HEADS = 8
PEER_NKEYS = 128
PEER_DQ = 128
PEER_TOPK = 16
PAGE_SIZE = 128
RMS_EPS = 1e-6
LANES = 128
SUBLANES = 8
VMEM_LIMIT = 56 * 1024 * 1024
LOG2E = math.log2(math.e)

NT_DIMS = (((1,), (1,)), ((), ()))
TN_DIMS = (((0,), (0,)), ((), ()))
NEG_INF = float("-inf")


def _cparams(sem):
    return pltpu.CompilerParams(dimension_semantics=sem, vmem_limit_bytes=VMEM_LIMIT)


def _rms_kernel(xa_ref, xb_ref, g_ref, o_ref, *, na):
    def norm(x_ref):
        x = x_ref[...]
        ms = jnp.mean(x * x, axis=-1, keepdims=True)
        o_ref[...] = (x * lax.rsqrt(ms + RMS_EPS) * g_ref[...]).astype(o_ref.dtype)

    i = pl.program_id(0)

    @pl.when(i < na)
    def _():
        norm(xa_ref)

    @pl.when(i >= na)
    def _():
        norm(xb_ref)


def rms_norm(xa, xb, g, tm, out_dtype=BF16):
    d = xa.shape[1]
    na, nb = xa.shape[0] // tm, xb.shape[0] // tm
    return pl.pallas_call(
        functools.partial(_rms_kernel, na=na),
        grid=(na + nb,),
        in_specs=[pl.BlockSpec((tm, d), lambda i: (jnp.minimum(i, na - 1), 0)),
                  pl.BlockSpec((tm, d), lambda i: (jnp.maximum(i - na, 0), 0)),
                  pl.BlockSpec((1, d), lambda i: (0, 0))],
        out_specs=pl.BlockSpec((tm, d), lambda i: (i, 0)),
        out_shape=jax.ShapeDtypeStruct(((na + nb) * tm, d), out_dtype),
        compiler_params=_cparams(("arbitrary",)),
        name="rms_norm",
    )(xa, xb, g.reshape(1, d))


def _mm_wt_kernel(x_ref, wt_ref, o_ref):
    o_ref[...] = lax.dot_general(x_ref[...], wt_ref[...], NT_DIMS, preferred_element_type=F32)


def matmul_wt(x, wt, tm, tn, name="matmul_wt"):
    n, k = x.shape
    m = wt.shape[0]
    return pl.pallas_call(
        _mm_wt_kernel,
        grid=(n // tm, m // tn),
        in_specs=[pl.BlockSpec((tm, k), lambda i, j: (i, 0)),
                  pl.BlockSpec((tn, k), lambda i, j: (j, 0))],
        out_specs=pl.BlockSpec((tm, tn), lambda i, j: (i, j)),
        out_shape=jax.ShapeDtypeStruct((n, m), F32),
        compiler_params=_cparams(("parallel", "parallel")),
        name=name,
    )(x, wt)


def _mm2_res_kernel(x1_ref, x2_ref, w_ref, r_ref, o_ref):
    k1 = x1_ref.shape[1]
    o_ref[...] = (r_ref[...] + jnp.dot(x1_ref[...], w_ref[:k1, :], preferred_element_type=F32)
                  + jnp.dot(x2_ref[...], w_ref[k1:, :], preferred_element_type=F32))


def matmul2_res(x1, x2, w, res, tm, tn, name="matmul2_res"):
    n, k1 = x1.shape
    k2 = x2.shape[1]
    m = w.shape[1]
    return pl.pallas_call(
        _mm2_res_kernel,
        grid=(n // tm, m // tn),
        in_specs=[pl.BlockSpec((tm, k1), lambda i, j: (i, 0)),
                  pl.BlockSpec((tm, k2), lambda i, j: (i, 0)),
                  pl.BlockSpec((k1 + k2, tn), lambda i, j: (0, j)),
                  pl.BlockSpec((tm, tn), lambda i, j: (i, j))],
        out_specs=pl.BlockSpec((tm, tn), lambda i, j: (i, j)),
        out_shape=jax.ShapeDtypeStruct((n, m), F32),
        compiler_params=_cparams(("parallel", "parallel")),
        name=name,
    )(x1, x2, w, res)


def _mm_nt_kernel(a_ref, b_ref, o_ref):
    o_ref[...] = lax.dot_general(a_ref[...], b_ref[...], NT_DIMS, preferred_element_type=F32)


def matmul_nt(a, b, tm, tn, name="matmul_nt"):
    m, k = a.shape
    n = b.shape[0]
    return pl.pallas_call(
        _mm_nt_kernel,
        grid=(n // tn, m // tm),
        in_specs=[pl.BlockSpec((tm, k), lambda j, i: (i, 0)),
                  pl.BlockSpec((tn, k), lambda j, i: (j, 0))],
        out_specs=pl.BlockSpec((tm, tn), lambda j, i: (i, j)),
        out_shape=jax.ShapeDtypeStruct((m, n), F32),
        compiler_params=_cparams(("parallel", "parallel")),
        name=name,
    )(a, b)


def _mm_acc_kernel(a_ref, b_ref, o_ref):
    @pl.when(pl.program_id(2) == 0)
    def _():
        o_ref[...] = jnp.zeros_like(o_ref)

    tm = o_ref.shape[0]
    rc = min(tm, 1024)
    for r in range(tm // rc):
        o_ref[r * rc:(r + 1) * rc, :] += jnp.dot(a_ref[r * rc:(r + 1) * rc, :], b_ref[...],
                                                 preferred_element_type=F32)


def matmul_kacc(a, b, tm, tn, tk, name="matmul_kacc"):
    m, k = a.shape
    n = b.shape[1]
    return pl.pallas_call(
        _mm_acc_kernel,
        grid=(m // tm, n // tn, k // tk),
        in_specs=[pl.BlockSpec((tm, tk), lambda i, j, kk: (i, kk)),
                  pl.BlockSpec((tk, tn), lambda i, j, kk: (kk, j))],
        out_specs=pl.BlockSpec((tm, tn), lambda i, j, kk: (i, j)),
        out_shape=jax.ShapeDtypeStruct((m, n), F32),
        compiler_params=_cparams(("parallel", "parallel", "arbitrary")),
        name=name,
    )(a, b)


GLA_SUB = 16


def _log_sigmoid(x):
    return jnp.minimum(x, 0.0) - jnp.log1p(jnp.exp(-jnp.abs(x)))


def _gla_chunk(q, k, v, r, ga, wa, ba, gain, sts, *, chunk, valid, dk, dv):
    nh = GLA_HEADS
    hk = [slice(h * dk, (h + 1) * dk) for h in range(nh)]
    hv = [slice(h * dv, (h + 1) * dv) for h in range(nh)]
    q = q * (dk ** -0.5)
    pre = jnp.dot(ga, wa, preferred_element_type=F32, precision=lax.Precision.HIGHEST) + ba
    g = _log_sigmoid(pre) * (1.0 / GLA_GATE_TEMP)
    if valid < chunk:
        g = jnp.where(lax.broadcasted_iota(jnp.int32, (chunk, 1), 0) < valid, g, 0.0)
    ri = lax.broadcasted_iota(jnp.int32, (chunk, chunk), 0)
    ci = lax.broadcasted_iota(jnp.int32, (chunk, chunk), 1)
    tril = (ci <= ri).astype(F32)
    G = jnp.dot(tril, g, preferred_element_type=F32, precision=lax.Precision.HIGHEST)
    g_last = G[chunk - 1:chunk, :]

    qg = (q * jnp.exp(G)).astype(BF16)
    o = [lax.dot_general(qg[:, hk[h]], sts[h].astype(BF16), NT_DIMS, preferred_element_type=F32)
         for h in range(nh)]

    sub = min(GLA_SUB, chunk)
    lane = lax.broadcasted_iota(jnp.int32, (sub, chunk), 1)
    srow = lax.broadcasted_iota(jnp.int32, (sub, chunk), 0)
    a_rows = [[] for _ in range(nh)]
    for i in range(chunk // sub):
        lo = i * sub
        Gi = G[lo:lo + sub, :]
        qi = q[lo:lo + sub, :]
        if i > 0:
            ref_i = G[lo - 1:lo, :]
            qs = (qi * jnp.exp(Gi - ref_i)).astype(BF16)
            ks = (k * jnp.exp(jnp.minimum(ref_i - G, 0.0))).astype(BF16)
            a_i = [jnp.where(lane < lo, lax.dot_general(qs[:, hk[h]], ks[:, hk[h]], NT_DIMS,
                                                        preferred_element_type=F32), 0.0) for h in range(nh)]
        else:
            a_i = [jnp.zeros((sub, chunk), F32) for _ in range(nh)]
        for s in range(sub):
            w = qi * jnp.exp(jnp.minimum(Gi - G[lo + s:lo + s + 1, :], 0.0)) * k[lo + s:lo + s + 1, :]
            for h in range(nh):
                col = jnp.sum(w[:, hk[h]], axis=1, keepdims=True)
                a_i[h] = jnp.where(lane == lo + s, col, a_i[h])
        for h in range(nh):
            a_rows[h].append(jnp.where(lane <= lo + srow, a_i[h], 0.0))

    kd = (k * jnp.exp(g_last - G)).astype(BF16)
    decay = jnp.exp(g_last)
    vb = v.astype(BF16)
    rg = r * jax.nn.sigmoid(r)
    outs, sts_new = [], []
    for h in range(nh):
        a_mat = a_rows[h][0] if len(a_rows[h]) == 1 else jnp.concatenate(a_rows[h], axis=0)
        oh = o[h] + jnp.dot(a_mat.astype(BF16), vb[:, hv[h]], preferred_element_type=F32)
        upd = lax.dot_general(vb[:, hv[h]], kd[:, hk[h]], TN_DIMS, preferred_element_type=F32)
        sts_new.append(sts[h] * decay[:, hk[h]] + upd)
        oh = oh[:valid, :]
        ms = jnp.mean(oh * oh, axis=-1, keepdims=True)
        outs.append(oh * lax.rsqrt(ms + RMS_EPS) * gain[:, hv[h]] * rg[:, hv[h]])
    return outs, sts_new


def _gla_kernel(*refs, chunk, valid, has_s0, dk, dv):
    if has_s0:
        (q_ref, k_ref, v_ref, r_ref, ga_ref, wa_ref, ba_ref, gain_ref, s0_ref, o_ref, s_ref, st_scr) = refs
    else:
        (q_ref, k_ref, v_ref, r_ref, ga_ref, wa_ref, ba_ref, gain_ref, o_ref, s_ref, st_scr) = refs
        s0_ref = None
    c = pl.program_id(1)
    nc = pl.num_programs(1)

    @pl.when(c == 0)
    def _():
        for h in range(GLA_HEADS):
            if has_s0:
                st_scr[h] = s0_ref[h].T
            else:
                st_scr[h] = jnp.zeros((dv, dk), F32)

    def rows(x):
        if valid < chunk:
            x = jnp.concatenate([x, jnp.zeros((chunk - valid, x.shape[1]), x.dtype)], axis=0)
        return x

    outs, sts_new = _gla_chunk(rows(q_ref[...]), rows(k_ref[...]), rows(v_ref[...]), r_ref[...],
                               rows(ga_ref[...]), wa_ref[...], ba_ref[...], gain_ref[...],
                               [st_scr[h] for h in range(GLA_HEADS)], chunk=chunk, valid=valid, dk=dk, dv=dv)
    for h in range(GLA_HEADS):
        st_scr[h] = sts_new[h]
        o_ref[:, h * dv:(h + 1) * dv] = outs[h].astype(o_ref.dtype)

    @pl.when(c == nc - 1)
    def _():
        for h in range(GLA_HEADS):
            s_ref[h] = sts_new[h].T


def gla(p, ga, wa2p, b_a, gain, s0, *, nbatch, seqlen, chunk, valid, row0, out_cols, out_dtype, dk, dv, name):
    nc = seqlen // valid
    rb0 = row0 // valid
    h_n = GLA_HEADS
    qk_w, v_w = h_n * dk, h_n * dv
    assert v_w == 2 * qk_w
    has_s0 = s0 is not None

    def rowblk(b, c):
        return rb0 + b * nc + c

    in_specs = [
        pl.BlockSpec((valid, qk_w), lambda b, c: (rowblk(b, c), 0)),
        pl.BlockSpec((valid, qk_w), lambda b, c: (rowblk(b, c), 1)),
        pl.BlockSpec((valid, v_w), lambda b, c: (rowblk(b, c), 1)),
        pl.BlockSpec((valid, v_w), lambda b, c: (rowblk(b, c), 2)),
        pl.BlockSpec((valid, LANES), lambda b, c: (rowblk(b, c), 0)),
        pl.BlockSpec((LANES, qk_w), lambda b, c: (0, 0)),
        pl.BlockSpec((1, qk_w), lambda b, c: (0, 0)),
        pl.BlockSpec((1, v_w), lambda b, c: (0, 0)),
    ]
    args = [p, p, p, p, ga, wa2p, b_a, gain]
    if has_s0:
        in_specs.append(pl.BlockSpec((None, h_n, dk, dv), lambda b, c: (b, 0, 0, 0)))
        args.append(s0)
    kern = functools.partial(_gla_kernel, chunk=chunk, valid=valid, has_s0=has_s0, dk=dk, dv=dv)
    return pl.pallas_call(
        kern,
        grid=(nbatch, nc),
        in_specs=in_specs,
        out_specs=[pl.BlockSpec((valid, v_w), lambda b, c: (b * nc + c, 0)),
                   pl.BlockSpec((None, h_n, dk, dv), lambda b, c: (b, 0, 0, 0))],
        out_shape=[jax.ShapeDtypeStruct((nbatch * seqlen, out_cols), out_dtype),
                   jax.ShapeDtypeStruct((nbatch, h_n, dk, dv), F32)],
        scratch_shapes=[pltpu.VMEM((h_n, dv, dk), F32)],
        compiler_params=_cparams(("parallel", "arbitrary")),
        name=name,
    )(*args)


def _attn_prompt_kernel(qi_ref, ki_ref, slopes_ref, lam_ref, q_ref, k_ref, v_ref, gain_ref, o_ref,
                        m_scr, l_scr, acc_scr, *, tq, tk, dk, dv, out_scale):
    h = pl.program_id(1)
    step_id = pl.program_id(2)
    qi = qi_ref[step_id]
    ki = ki_ref[step_id]

    @pl.when(ki == 0)
    def _():
        m_scr[...] = jnp.full_like(m_scr, NEG_INF)
        l_scr[...] = jnp.zeros_like(l_scr)
        acc_scr[...] = jnp.zeros_like(acc_scr)

    def step(masked):
        krel = (ki * tk - qi * tq + lax.broadcasted_iota(jnp.int32, (1, tk), 1)).astype(F32)
        kbias = (slopes_ref[h] * LOG2E) * krel
        vb = v_ref[...].astype(BF16)
        if masked:
            keep = (lax.broadcasted_iota(jnp.int32, (tq, tk), 0)
                    >= lax.broadcasted_iota(jnp.int32, (tq, tk), 1))
        streams = range(2)
        qb = (q_ref[...] * (dk ** -0.5 * LOG2E)).astype(BF16)
        kb = k_ref[...].astype(BF16)
        s = [lax.dot_general(qb[:, c * dk:(c + 1) * dk], kb[:, c * dk:(c + 1) * dk], NT_DIMS,
                             preferred_element_type=F32) + kbias for c in streams]
        if masked:
            s = [jnp.where(keep, sc, NEG_INF) for sc in s]
        m_prev = [m_scr[c] for c in streams]
        m_new = [jnp.maximum(m_prev[c], jnp.max(s[c], axis=-1, keepdims=True)) for c in streams]
        alpha = [jnp.exp2(m_prev[c] - m_new[c]) for c in streams]
        p = [jnp.exp2(s[c] - jnp.tile(m_new[c], (1, tk // LANES))) for c in streams]
        psum = [jnp.sum(p[c], axis=-1, keepdims=True) for c in streams]
        pv = [jnp.dot(p[c].astype(BF16), vb, preferred_element_type=F32) for c in streams]
        for c in streams:
            l_scr[c] = alpha[c] * l_scr[c] + psum[c]
            acc_scr[c] = jnp.tile(alpha[c], (1, dv // LANES)) * acc_scr[c] + pv[c]
            m_scr[c] = m_new[c]

    @pl.when(ki < qi)
    def _():
        step(False)

    @pl.when(ki == qi)
    def _():
        step(True)
        lam = lam_ref[0]
        reps = (1, dv // LANES)
        o = acc_scr[0] / jnp.tile(l_scr[0], reps) - lam * (acc_scr[1] / jnp.tile(l_scr[1], reps))
        ms = jnp.mean(o * o, axis=-1, keepdims=True)
        o = o * lax.rsqrt(ms + RMS_EPS) * gain_ref[...] * out_scale
        o_ref[...] = o.astype(o_ref.dtype)


def attn_prompt(p, slopes, lam, gain, *, nbatch, seqlen, tq, col_q, col_k, col_v, dk, dv, out_scale):
    nq = seqlen // tq
    pairs = [(i, j) for i in range(nq) for j in range(i + 1)]
    qi_tab = jnp.asarray([i for i, _ in pairs], jnp.int32)
    ki_tab = jnp.asarray([j for _, j in pairs], jnp.int32)
    kern = functools.partial(_attn_prompt_kernel, tq=tq, tk=tq, dk=dk, dv=dv, out_scale=out_scale)
    smem = pl.BlockSpec(memory_space=pltpu.SMEM)
    grid_spec = pltpu.PrefetchScalarGridSpec(
        num_scalar_prefetch=2,
        grid=(nbatch, DIFF_HEADS, len(pairs)),
        in_specs=[smem, smem,
                  pl.BlockSpec((tq, 2 * dk), lambda b, h, s, qt, kt: (b * nq + qt[s], col_q // (2 * dk) + h)),
                  pl.BlockSpec((tq, 2 * dk), lambda b, h, s, qt, kt: (b * nq + kt[s], col_k // (2 * dk) + h)),
                  pl.BlockSpec((tq, dv), lambda b, h, s, qt, kt: (b * nq + kt[s], col_v // dv + h)),
                  pl.BlockSpec((1, dv), lambda b, h, s, qt, kt: (0, 0))],
        out_specs=pl.BlockSpec((tq, dv), lambda b, h, s, qt, kt: (b * nq + qt[s], h)),
        scratch_shapes=[pltpu.VMEM((2, tq, LANES), F32), pltpu.VMEM((2, tq, LANES), F32),
                        pltpu.VMEM((2, tq, dv), F32)],
    )
    return pl.pallas_call(
        kern,
        grid_spec=grid_spec,
        out_shape=jax.ShapeDtypeStruct((nbatch * seqlen, DIFF_HEADS * dv), BF16),
        compiler_params=_cparams(("parallel", "parallel", "arbitrary")),
        name="attn_prompt",
    )(qi_tab, ki_tab, slopes, lam, p, p, p, gain)


def _attn_sample_kernel(pt_ref, lam_ref, q_ref, kn_ref, vn_ref, gain_ref, c0n_ref, c0s_ref, scol_ref, *rest,
                        pg, tnew, past_len, dk, dv, out_scale):
    k_refs = rest[:pg]
    v_refs = rest[pg:2 * pg]
    o_ref = rest[2 * pg]
    qt_scr, m_scr, l_scr, acc_scr = rest[2 * pg + 1:]
    j = pl.program_id(1)
    nj = pl.num_programs(1)
    nh = DIFF_HEADS
    half = nh * tnew
    page_lanes = PAGE_SIZE * nh

    @pl.when(j == 0)
    def _():
        s_parts = []
        for c in range(2):
            cols = [slice((2 * h + c) * dk, (2 * h + c + 1) * dk) for h in range(nh)]
            qt = jnp.concatenate([q_ref[:, cs] for cs in cols], axis=0) * (dk ** -0.5 * LOG2E)
            qt_scr[c] = qt.astype(BF16)
            kn = jnp.concatenate([kn_ref[:, cs] for cs in cols], axis=0)
            s_parts.append(lax.dot_general(qt, kn, NT_DIMS, preferred_element_type=F32))
        s = jnp.concatenate(s_parts, axis=0) - c0n_ref[...]
        m = jnp.max(s, axis=-1, keepdims=True)
        p = jnp.exp2(s - m)
        vn = jnp.concatenate([vn_ref[:, h * dv:(h + 1) * dv] for h in range(nh)], axis=0)
        m_scr[...] = m
        l_scr[...] = jnp.sum(p, axis=-1, keepdims=True)
        acc_scr[...] = jnp.dot(p, vn, preferred_element_type=F32)

    s_parts = []
    for c in range(2):
        qt = qt_scr[c]
        s_parts.append(jnp.concatenate(
            [lax.dot_general(qt, k_refs[i][pl.ds(c, page_lanes, stride=2), :].astype(BF16), NT_DIMS,
                             preferred_element_type=F32) for i in range(pg)], axis=-1))
    s = jnp.concatenate(s_parts, axis=0) - c0s_ref[...]
    addcol = scol_ref[...] * (j * (pg * PAGE_SIZE) - past_len).astype(F32)
    m_prev = m_scr[...]
    m_new = jnp.maximum(m_prev, jnp.max(s, axis=-1, keepdims=True) + addcol)
    alpha = jnp.exp2(m_prev - m_new)
    p = jnp.exp2(s + (addcol - m_new))
    l_scr[...] = alpha * l_scr[...] + jnp.sum(p, axis=-1, keepdims=True)
    pb = p.astype(BF16)
    acc = alpha * acc_scr[...]
    for i in range(pg):
        vi = v_refs[i][...].reshape(page_lanes, dv).astype(BF16)
        acc = acc + jnp.dot(pb[:, i * page_lanes:(i + 1) * page_lanes], vi, preferred_element_type=F32)
    acc_scr[...] = acc
    m_scr[...] = m_new

    @pl.when(j == nj - 1)
    def _():
        lam = lam_ref[0]
        on = acc / l_scr[...]
        for h in range(nh):
            o = on[h * tnew:(h + 1) * tnew, :] - lam * on[half + h * tnew:half + (h + 1) * tnew, :]
            ms = jnp.mean(o * o, axis=-1, keepdims=True)
            o_ref[:, h * dv:(h + 1) * dv] = o * lax.rsqrt(ms + RMS_EPS) * gain_ref[...] * out_scale


def _sample_bias_tiles(slopes, tnew, pg):
    nh = DIFF_HEADS
    r = np.arange(2 * nh * tnew)
    rh, rq = (r // tnew) % nh, r % tnew
    sl = np.asarray(slopes, np.float64)[rh] * LOG2E
    ln = np.arange(nh * tnew)
    lh, lt = ln // tnew, ln % tnew
    ok = (lh[None, :] == rh[:, None]) & (lt[None, :] <= rq[:, None])
    c0n = np.where(ok, sl[:, None] * (rq[:, None] - lt[None, :]), np.inf)
    ls = np.arange(pg * PAGE_SIZE * nh)
    lh, lt = ls % nh, ls // nh
    ok = lh[None, :] == rh[:, None]
    c0s = np.where(ok, sl[:, None] * (rq[:, None] - lt[None, :]), np.inf)
    return (jnp.asarray(c0n, F32), jnp.asarray(c0s, F32), jnp.asarray(sl[:, None], F32))


def attn_sample(p, cache_k, cache_v, page_table, slopes, lam, gain, *, nbatch, tnew, row0, pg,
                colblk_q, colblk_k, colblk_v, dk, dv, out_scale):
    n_pages = page_table.shape[1]
    nj = n_pages // pg
    nh = DIFF_HEADS
    width = nh * dv
    rb0 = row0 // tnew
    rows = 2 * nh * tnew
    c0n, c0s, scol = _sample_bias_tiles(slopes, tnew, pg)
    kern = functools.partial(_attn_sample_kernel, pg=pg, tnew=tnew, past_len=n_pages * PAGE_SIZE,
                             dk=dk, dv=dv, out_scale=out_scale)
    smem = pl.BlockSpec(memory_space=pltpu.SMEM)

    def kpage_spec(i):
        return pl.BlockSpec((None, PAGE_SIZE * nh * 2, dk), lambda b, j, pt: (pt[b, j * pg + i], 0, 0))

    def vpage_spec(i):
        return pl.BlockSpec((None, PAGE_SIZE, nh, dv), lambda b, j, pt: (pt[b, j * pg + i], 0, 0, 0))

    in_specs = [smem,
                pl.BlockSpec((tnew, width), lambda b, j, pt: (rb0 + b, colblk_q)),
                pl.BlockSpec((tnew, width), lambda b, j, pt: (rb0 + b, colblk_k)),
                pl.BlockSpec((tnew, width), lambda b, j, pt: (rb0 + b, colblk_v)),
                pl.BlockSpec((1, dv), lambda b, j, pt: (0, 0)),
                pl.BlockSpec(c0n.shape, lambda b, j, pt: (0, 0)),
                pl.BlockSpec(c0s.shape, lambda b, j, pt: (0, 0), pipeline_mode=pl.Buffered(1)),
                pl.BlockSpec((rows, 1), lambda b, j, pt: (0, 0))]
    in_specs += [kpage_spec(i) for i in range(pg)] + [vpage_spec(i) for i in range(pg)]
    grid_spec = pltpu.PrefetchScalarGridSpec(
        num_scalar_prefetch=1,
        grid=(nbatch, nj),
        in_specs=in_specs,
        out_specs=pl.BlockSpec((tnew, width), lambda b, j, pt: (b, 0)),
        scratch_shapes=[pltpu.VMEM((2, nh * tnew, dk), BF16),
                        pltpu.VMEM((rows, 1), F32),
                        pltpu.VMEM((rows, 1), F32),
                        pltpu.VMEM((rows, dv), F32)],
    )
    return pl.pallas_call(
        kern,
        grid_spec=grid_spec,
        out_shape=jax.ShapeDtypeStruct((nbatch * tnew, width), F32),
        compiler_params=_cparams(("parallel", "arbitrary")),
        name="attn_sample",
    )(page_table, lam, p, p, p, gain, c0n, c0s, scol, *([cache_k] * pg), *([cache_v] * pg))


N_TOP = PEER_TOPK + 1


def _top_values(x, n):
    cur = jnp.max(x, axis=0, keepdims=True)
    vals = [cur]
    for _ in range(n - 1):
        cur = jnp.max(jnp.where(x < cur, x, NEG_INF), axis=0, keepdims=True)
        vals.append(cur)
    return vals


_CAND_PAIRS = [(a, b) for a in range(N_TOP) for b in range(N_TOP) if (a + 1) * (b + 1) <= N_TOP]
_CAND_ROWS = -(-len(_CAND_PAIRS) // SUBLANES) * SUBLANES


def _route_kernel(qt_ref, k1_ref, k2_ref, e1_ref, e2_ref, th_ref, cand_scr):
    half = PEER_DQ // 2
    tn = qt_ref.shape[1]
    cand_scr[...] = jnp.full_like(cand_scr, NEG_INF)
    for h in range(PEER_HEADS):
        q1 = qt_ref[h * PEER_DQ:h * PEER_DQ + half, :]
        q2 = qt_ref[h * PEER_DQ + half:(h + 1) * PEER_DQ, :]
        s1 = jnp.dot(k1_ref[...], q1, preferred_element_type=F32, precision=lax.Precision.HIGHEST)
        s2 = jnp.dot(k2_ref[...], q2, preferred_element_type=F32, precision=lax.Precision.HIGHEST)
        t1 = _top_values(s1, N_TOP)
        t2 = _top_values(s2, N_TOP)
        for i, (a, b) in enumerate(_CAND_PAIRS):
            cand_scr[i:i + 1, :] = t1[a] + t2[b]
        c = _top_values(cand_scr[...], N_TOP)
        m = c[0]
        z = jnp.zeros_like(m)
        for i in range(PEER_TOPK):
            z = z + jnp.exp(c[i] - m)
        scale = 0.5 / z
        e1 = jnp.exp(s1 - t1[0])
        e2 = jnp.exp(s2 - t2[0]) * scale
        thr = jnp.exp(0.5 * (c[PEER_TOPK - 1] + c[PEER_TOPK]) - m) * scale
        for cb in range(tn // LANES):
            e1_ref[h, cb] = e1[:, cb * LANES:(cb + 1) * LANES]
            e2_ref[h, cb] = e2[:, cb * LANES:(cb + 1) * LANES]
            th_ref[cb, h:h + 1, :] = thr[:, cb * LANES:(cb + 1) * LANES]


def peer_route(qt, keys1, keys2, tn):
    n = qt.shape[1]
    nk = PEER_NKEYS
    nb = tn // LANES
    tile4 = pl.BlockSpec((PEER_HEADS, nb, nk, LANES), lambda i: (0, i, 0, 0))
    shape4 = jax.ShapeDtypeStruct((PEER_HEADS, n // LANES, nk, LANES), F32)
    return pl.pallas_call(
        _route_kernel,
        grid=(n // tn,),
        in_specs=[pl.BlockSpec((PEER_HEADS * PEER_DQ, tn), lambda i: (0, i)),
                  pl.BlockSpec((nk, PEER_DQ // 2), lambda i: (0, 0)),
                  pl.BlockSpec((nk, PEER_DQ // 2), lambda i: (0, 0))],
        out_specs=[tile4, tile4, pl.BlockSpec((nb, PEER_HEADS, LANES), lambda i: (i, 0, 0))],
        out_shape=[shape4, shape4, jax.ShapeDtypeStruct((n // LANES, PEER_HEADS, LANES), F32)],
        scratch_shapes=[pltpu.VMEM((_CAND_ROWS, tn), F32)],
        compiler_params=_cparams(("parallel",)),
        name="peer_route",
    )(qt, keys1, keys2)


_GELU_C = math.sqrt(2.0 / math.pi)


def _peer_act_kernel(x_ref, u_ref, e2_ref, e1_ref, th_ref, o_ref, pre_scr):
    te, tt = o_ref.shape
    nk = PEER_NKEYS
    pre_scr[...] = lax.dot_general(u_ref[...], x_ref[...], NT_DIMS, preferred_element_type=F32)

    def row_group(r, carry):
        r0 = pl.multiple_of(r * nk, nk)
        for cb in range(tt // LANES):
            cs = slice(cb * LANES, (cb + 1) * LANES)
            w = jnp.zeros((nk, LANES), F32)
            for h in range(PEER_HEADS):
                e1 = jnp.tile(jnp.broadcast_to(e1_ref[h, cb, pl.ds(r, 1), :], (SUBLANES, LANES)),
                              (nk // SUBLANES, 1))
                th = jnp.tile(jnp.broadcast_to(th_ref[cb, h:h + 1, :], (SUBLANES, LANES)), (nk // SUBLANES, 1))
                p = e2_ref[h, cb] * e1
                w = w + jnp.where(p > th, p, 0.0)
            x = pre_scr[pl.ds(r0, nk), cs]
            g = x * (1.0 + jnp.tanh(_GELU_C * (x + 0.044715 * (x * x * x))))
            o_ref[pl.ds(r0, nk), cs] = (g * w).astype(o_ref.dtype)
        return carry

    lax.fori_loop(0, te // nk, row_group, 0)


def peer_act(xn, u, e1t, e2t, tht, *, tt, te):
    n, d = xn.shape
    ne = u.shape[0]
    rows = te // PEER_NKEYS
    nb = tt // LANES
    return pl.pallas_call(
        _peer_act_kernel,
        grid=(n // tt, ne // te),
        in_specs=[pl.BlockSpec((tt, d), lambda t, e: (t, 0)),
                  pl.BlockSpec((te, d), lambda t, e: (e, 0)),
                  pl.BlockSpec((PEER_HEADS, nb, PEER_NKEYS, LANES), lambda t, e: (0, t, 0, 0)),
                  pl.BlockSpec((PEER_HEADS, nb, rows, LANES), lambda t, e: (0, t, e, 0)),
                  pl.BlockSpec((nb, PEER_HEADS, LANES), lambda t, e: (t, 0, 0))],
        out_specs=pl.BlockSpec((te, tt), lambda t, e: (e, t)),
        out_shape=jax.ShapeDtypeStruct((ne, n), BF16),
        scratch_shapes=[pltpu.VMEM((te, tt), F32)],
        compiler_params=_cparams(("parallel", "arbitrary")),
        name="peer_act",
    )(xn, u, e2t, e1t, tht)


def _final_kernel(h_ref, pt_ref, g_ref, o_ref):
    x = h_ref[...] + pt_ref[...].T
    ms = jnp.mean(x * x, axis=-1, keepdims=True)
    o_ref[...] = x * lax.rsqrt(ms + RMS_EPS) * g_ref[...]


def final_norm(h, peer_t, g, *, col0, tm):
    nrows, d = h.shape
    cb0 = col0 // tm
    return pl.pallas_call(
        _final_kernel,
        grid=(nrows // tm,),
        in_specs=[pl.BlockSpec((tm, d), lambda i: (i, 0)),
                  pl.BlockSpec((d, tm), lambda i: (0, cb0 + i)),
                  pl.BlockSpec((1, d), lambda i: (0, 0))],
        out_specs=pl.BlockSpec((tm, d), lambda i: (i, 0)),
        out_shape=jax.ShapeDtypeStruct((nrows, d), F32),
        compiler_params=_cparams(("parallel",)),
        name="final_norm",
    )(h, peer_t, g.reshape(1, d))


def _tiles(n, n_p, n_s):
    def pick(cands, m=n):
        for c in cands:
            if m % c == 0:
                return c
        raise ValueError(f"no tile for {m}")
    both = math.gcd(n_p, n_s)
    return dict(
        rms_tm=pick((256, 128, 64, 32, 16), both),
        mm_tm=pick((768, 512, 256, 128, 64, 32, 16)),
        proj_tm=pick((1408, 768, 512, 256, 128, 64, 32, 16)),
        mm_tm_p=pick((1024, 512, 256, 128, 64, 32, 16), n_p),
        mm_tm_s=pick((256, 128, 64, 32, 16), n_s),
        peer_tt=pick((768, 512, 256, 128)),
        route_tn=pick((256, 128)),
        final_tm=pick((256, 128), both),
    )


def _alibi_slopes(nheads):
    return [2.0 ** (-8.0 * (i + 1) / nheads) for i in range(nheads)]


def kernel(x_prompt, x_sample, cache_k, cache_v, state_gla, page_table, norm1_gain, w_in, w_a2, b_a, gla_gain,
           lambda_q1, lambda_k1, lambda_q2, lambda_k2, diff_gain, w_o, norm2_gain, peer_wq, peer_keys1,
           peer_keys2, peer_u, peer_v, final_gain):
    depth = w_in.shape[0]
    assert depth == 1
    B, T, D = x_prompt.shape
    DB, TS, _ = x_sample.shape
    n_p, n_s = B * T, DB * TS
    n = n_p + n_s
    gla_dv = D // (2 * GLA_HEADS)
    gla_dk = gla_dv // 2
    diff_dv = D // (2 * DIFF_HEADS)
    diff_dk = diff_dv // 2
    gqk_w = GLA_HEADS * gla_dk
    gv_w = GLA_HEADS * gla_dv
    dqk_w = DIFF_HEADS * 2 * diff_dk
    dv_w = DIFF_HEADS * diff_dv
    g_w = 2 * gqk_w + 2 * gv_w
    tl = _tiles(n, n_p, n_s)
    l = 0
    lam_init = 0.8 - 0.6 * math.exp(-0.3 * l)
    xp = x_prompt.reshape(n_p, D)
    xs = x_sample.reshape(n_s, D)

    wt = w_in[l].T
    wt_g = wt[:g_w].astype(BF16)
    wt_d = wt[g_w + GLA_LOWRANK:].astype(BF16)
    wt_ga = jnp.pad(wt[g_w:g_w + GLA_LOWRANK], ((0, LANES - GLA_LOWRANK), (0, 0))).astype(BF16)
    wa2p = jnp.pad(w_a2[l], ((0, LANES - GLA_LOWRANK), (0, 0)))
    w_ob = w_o[l].astype(BF16)
    f32 = F32
    lam = (jnp.exp(jnp.sum(lambda_q1[l].astype(f32) * lambda_k1[l].astype(f32)))
           - jnp.exp(jnp.sum(lambda_q2[l].astype(f32) * lambda_k2[l].astype(f32))) + lam_init).reshape(1)
    slopes = _alibi_slopes(DIFF_HEADS)

    xn = rms_norm(xp, xs, norm1_gain[l], tl["rms_tm"])
    proj_g = matmul_wt(xn, wt_g, tl["proj_tm"], 512, name="proj_gla")
    proj_d = matmul_wt(xn, wt_d, tl["proj_tm"], 512, name="proj_diff")
    ga = matmul_wt(xn, wt_ga, tl["mm_tm"], LANES, name="proj_gate")

    gla_args = (proj_g, ga, wa2p, b_a[l].reshape(1, -1), gla_gain[l].reshape(1, -1))
    og_p, s_p = gla(*gla_args, None, nbatch=B, seqlen=T, chunk=64, valid=64, row0=0, out_cols=gv_w,
                    out_dtype=BF16, dk=gla_dk, dv=gla_dv, name="gla_prompt")
    og_s, s_s = gla(*gla_args, state_gla[l], nbatch=DB, seqlen=TS, chunk=GLA_SUB, valid=TS, row0=n_p,
                    out_cols=gv_w, out_dtype=F32, dk=gla_dk, dv=gla_dv, name="gla_sample")

    out_scale = 1.0 - lam_init
    dgain = diff_gain[l].reshape(1, -1)
    od_p = attn_prompt(proj_d, jnp.asarray(slopes, F32), lam, dgain, nbatch=B, seqlen=T, tq=min(512, T),
                       col_q=0, col_k=dqk_w, col_v=2 * dqk_w, dk=diff_dk, dv=diff_dv, out_scale=out_scale)
    n_pool = cache_k.shape[1]
    ck = cache_k[l].reshape(n_pool, PAGE_SIZE * DIFF_HEADS * 2, diff_dk)
    od_s = attn_sample(proj_d, ck, cache_v[l], page_table, slopes, lam, dgain, nbatch=DB, tnew=TS, row0=n_p,
                       pg=min(8, page_table.shape[1]), colblk_q=0, colblk_k=1, colblk_v=2, dk=diff_dk,
                       dv=diff_dv, out_scale=out_scale)

    h_p = matmul2_res(og_p, od_p, w_ob, xp, tl["mm_tm_p"], 1024, name="out_proj")
    h_s = matmul2_res(og_s.astype(BF16), od_s.astype(BF16), w_ob, xs, tl["mm_tm_s"], 1024,
                      name="out_proj_sample")

    xn2 = rms_norm(h_p, h_s, norm2_gain[l], tl["rms_tm"])
    qt = matmul_nt(peer_wq[l].T.astype(BF16), xn2, PEER_HEADS * PEER_DQ, tl["peer_tt"], name="peer_query")
    e1t, e2t, tht = peer_route(qt, peer_keys1[l], peer_keys2[l], tl["route_tn"])
    act_t = peer_act(xn2, peer_u[l].astype(BF16), e1t, e2t, tht, tt=tl["peer_tt"], te=1024)
    peer_t = matmul_kacc(peer_v[l].T.astype(BF16), act_t, D, tl["peer_tt"], 1024, name="peer_out")

    y_p = final_norm(h_p, peer_t, final_gain, col0=0, tm=tl["final_tm"])
    y_s = final_norm(h_s, peer_t, final_gain, col0=n_p, tm=tl["final_tm"])

    k_p = proj_d[:n_p, dqk_w:2 * dqk_w].reshape(1, B, T, DIFF_HEADS, 2, diff_dk)
    v_p = proj_d[:n_p, 2 * dqk_w:].reshape(1, B, T, DIFF_HEADS, diff_dv)
    k_s = proj_d[n_p:, dqk_w:2 * dqk_w].reshape(1, DB, TS, DIFF_HEADS, 2, diff_dk)
    v_s = proj_d[n_p:, 2 * dqk_w:].reshape(1, DB, TS, DIFF_HEADS, diff_dv)
    return (y_p.reshape(B, T, D), y_s.reshape(DB, TS, D), k_p, v_p, s_p[None], k_s, v_s, s_s[None])
```
